```python
import math
import jax, jax.numpy as jnp
from jax import lax
import numpy as np

D_MODEL = 4096
BATCH = 1
SEQ = 16384
DEPTH = 2

HA = 16
QK_NOPE = 128
QK_ROPE = 64
V_DIM_A = 128
Q_LORA = 1024
KV_LORA = 512
ROPE_THETA = 10000.0
HB = 8
DH_B = 128
N_GROUPS = 8
EXP_PER_GROUP = 8
N_EXPERTS = N_GROUPS * EXP_PER_GROUP
TOP_K = 2
D_EXPERT = 384
MOE_BLOCK = 128
Q_BLOCK = 128
EPS = 1e-6

QA_W = HA * (QK_NOPE + QK_ROPE)
KVA_W = HA * (QK_NOPE + V_DIM_A)
WIDTH_A = HA * V_DIM_A
WIDTH_B = HB * 2 * DH_B
IN_WIDTHS = (Q_LORA, KV_LORA, QK_ROPE, WIDTH_B, WIDTH_B, WIDTH_B, D_MODEL, D_MODEL)
D_IN = sum(IN_WIDTHS)

kernel_name = "hybrid_mla_diffattn_hiermoe"


def rms_norm(x, g, eps=EPS):
    xf = x.astype(jnp.float32)
    y = xf * lax.rsqrt(jnp.mean(xf * xf, axis=-1, keepdims=True) + eps)
    return (y * g.astype(jnp.float32)).astype(x.dtype)


def split_indices(widths):
    idx, acc = [], 0
    for w in widths[:-1]:
        acc += w
        idx.append(acc)
    return idx


def apply_rope(x, pos):
    half = x.shape[-1] // 2
    inv = ROPE_THETA ** (-jnp.arange(half, dtype=jnp.float32) / half)
    ang = pos.astype(jnp.float32)[..., None] * inv
    cos = jnp.cos(ang)[:, :, None, :]
    sin = jnp.sin(ang)[:, :, None, :]
    xf = x.astype(jnp.float32)
    x1, x2 = xf[..., :half], xf[..., half:]
    return jnp.concatenate([x1 * cos - x2 * sin, x1 * sin + x2 * cos], axis=-1).astype(x.dtype)


def alibi_slopes(n):
    return jnp.asarray(2.0 ** (-8.0 * (np.arange(n) + 1) / n), dtype=jnp.float32)


def mla_attention(q, k, v):
    B, S, H, Dqk = q.shape
    Dv = v.shape[-1]
    nb = S // Q_BLOCK
    scale = Dqk ** -0.5
    kidx = jnp.arange(S)

    def block(i):
        start = i * Q_BLOCK
        qb = lax.dynamic_slice_in_dim(q, start, Q_BLOCK, axis=1)
        s = jnp.einsum('bqhd,bkhd->bhqk', qb, k, preferred_element_type=jnp.float32) * scale
        qidx = start + jnp.arange(Q_BLOCK)
        s = jnp.where(kidx[None, :] <= qidx[:, None], s, -jnp.inf)
        p = jax.nn.softmax(s, axis=-1).astype(v.dtype)
        return jnp.einsum('bhqk,bkhd->bqhd', p, v)

    out = lax.map(block, jnp.arange(nb))
    return jnp.moveaxis(out, 0, 1).reshape(B, S, H, Dv)


def diff_attention(q, k, v, pos, lam, slopes):
    B, S, H, _, d = q.shape
    Dv = v.shape[-1]
    nb = S // Q_BLOCK
    scale = d ** -0.5
    kidx = jnp.arange(S)

    def block(i):
        start = i * Q_BLOCK
        qb = lax.dynamic_slice_in_dim(q, start, Q_BLOCK, axis=1)
        pq = lax.dynamic_slice_in_dim(pos, start, Q_BLOCK, axis=1)
        s = jnp.einsum('bqhcd,bkhcd->bchqk', qb, k, preferred_element_type=jnp.float32) * scale
        dist = jnp.abs(pq[:, :, None] - pos[:, None, :]).astype(jnp.float32)
        s = s - slopes[None, None, :, None, None] * dist[:, None, None]
        qidx = start + jnp.arange(Q_BLOCK)
        s = jnp.where(kidx[None, :] <= qidx[:, None], s, -jnp.inf)
        p = jax.nn.softmax(s, axis=-1)
        a = p[:, 0] - lam * p[:, 1]
        return jnp.einsum('bhqk,bkhd->bqhd', a.astype(v.dtype), v)

    out = lax.map(block, jnp.arange(nb))
    return jnp.moveaxis(out, 0, 1).reshape(B, S, H, Dv)


def hier_moe(h, w_rg, b_rg, w_re, b_re, w1, w3, w2):
    B, S, D = h.shape
    T = B * S
    ht = h.reshape(T, D)
    hf = ht.astype(jnp.float32)
    g_prob = jax.nn.softmax(hf @ w_rg.astype(jnp.float32) + b_rg.astype(jnp.float32), axis=-1)
    p_g, g_idx = lax.top_k(g_prob, 1)
    e_logit = (hf @ w_re.astype(jnp.float32) + b_re.astype(jnp.float32)).reshape(T, N_GROUPS, EXP_PER_GROUP)
    e_logit = jnp.take_along_axis(e_logit, g_idx[:, :, None], axis=1)[:, 0]
    p_e, e_loc = lax.top_k(jax.nn.softmax(e_logit, axis=-1), TOP_K)
    gate = p_g * p_e / jnp.sum(p_e, axis=-1, keepdims=True)
    expert = g_idx * EXP_PER_GROUP + e_loc

    A = T * TOP_K
    flat_e = expert.reshape(A)
    flat_tok = (jnp.arange(A) // TOP_K).astype(jnp.int32)
    flat_w = gate.reshape(A)
    order = jnp.argsort(flat_e)
    se, stok, sw = flat_e[order], flat_tok[order], flat_w[order]
    counts = jnp.bincount(flat_e, length=N_EXPERTS)
    padded = (counts + MOE_BLOCK - 1) // MOE_BLOCK * MOE_BLOCK
    pad_end = jnp.cumsum(padded)
    pad_start = pad_end - padded
    seg_start = jnp.cumsum(counts) - counts
    dest = pad_start[se] + jnp.arange(A) - seg_start[se]
    NB = -(-A // MOE_BLOCK) + N_EXPERTS
    P = NB * MOE_BLOCK
    buf_tok = jnp.full((P,), T, dtype=jnp.int32).at[dest].set(stok)
    buf_w = jnp.zeros((P,), jnp.float32).at[dest].set(sw)
    blk_exp = jnp.clip(jnp.searchsorted(pad_end, jnp.arange(NB) * MOE_BLOCK, side='right'), 0, N_EXPERTS - 1)
    ht_pad = jnp.concatenate([ht, jnp.zeros((1, D), ht.dtype)], axis=0)

    def step(acc, blk):
        tok, w, e = blk
        xb = ht_pad[tok]
        yb = (jax.nn.silu(xb @ w1[e]) * (xb @ w3[e])) @ w2[e]
        return acc.at[tok].add(yb * w[:, None].astype(yb.dtype)), None

    acc, _ = lax.scan(step, jnp.zeros((T + 1, D), ht.dtype),
                      (buf_tok.reshape(NB, MOE_BLOCK), buf_w.reshape(NB, MOE_BLOCK), blk_exp))
    return acc[:T].reshape(B, S, D)


def setup_inputs(seed: int = 0) -> dict:
    key = jax.random.key(seed)
    ks = jax.random.split(key, 32)
    L = DEPTH

    def nrm(k, shape, scale):
        return jax.random.normal(k, shape, jnp.float32) * scale

    def gain(k, shape):
        return 1.0 + 0.05 * jax.random.normal(k, shape, jnp.float32)

    start = jax.random.randint(ks[1], (BATCH, 1), 0, 1024, dtype=jnp.int32)
    positions = (start + jnp.arange(SEQ, dtype=jnp.int32)[None, :]).astype(jnp.int32)
    return {
        "x": nrm(ks[0], (BATCH, SEQ, D_MODEL), 1.0),
        "positions": positions,
        "norm_attn": gain(ks[2], (L, D_MODEL)),
        "w_in": nrm(ks[3], (L, D_MODEL, D_IN), D_MODEL ** -0.5),
        "q_norm": gain(ks[4], (L, Q_LORA)),
        "w_uq": nrm(ks[5], (L, Q_LORA, QA_W), Q_LORA ** -0.5),
        "kv_norm": gain(ks[6], (L, KV_LORA)),
        "w_ukv": nrm(ks[7], (L, KV_LORA, KVA_W), KV_LORA ** -0.5),
        "lam_q1": nrm(ks[8], (L, DH_B), 0.1),
        "lam_k1": nrm(ks[9], (L, DH_B), 0.1),
        "lam_q2": nrm(ks[10], (L, DH_B), 0.1),
        "lam_k2": nrm(ks[11], (L, DH_B), 0.1),
        "subln": gain(ks[12], (L, 2 * DH_B)),
        "w_oa": nrm(ks[13], (L, WIDTH_A, D_MODEL), WIDTH_A ** -0.5),
        "w_ob": nrm(ks[14], (L, WIDTH_B, D_MODEL), WIDTH_B ** -0.5),
        "w_out": nrm(ks[15], (L, D_MODEL, D_MODEL), D_MODEL ** -0.5),
        "norm_ffn": gain(ks[16], (L, D_MODEL)),
        "w_router_g": nrm(ks[17], (L, D_MODEL, N_GROUPS), D_MODEL ** -0.5),
        "b_router_g": nrm(ks[18], (L, N_GROUPS), 0.01),
        "w_router_e": nrm(ks[19], (L, D_MODEL, N_EXPERTS), D_MODEL ** -0.5),
        "b_router_e": nrm(ks[20], (L, N_EXPERTS), 0.01),
        "w1": nrm(ks[21], (L, N_EXPERTS, D_MODEL, D_EXPERT), D_MODEL ** -0.5),
        "w3": nrm(ks[22], (L, N_EXPERTS, D_MODEL, D_EXPERT), D_MODEL ** -0.5),
        "w2": nrm(ks[23], (L, N_EXPERTS, D_EXPERT, D_MODEL), D_EXPERT ** -0.5),
        "norm_final": gain(ks[24], (D_MODEL,)),
    }


def reference(x, positions, norm_attn, w_in, q_norm, w_uq, kv_norm, w_ukv, lam_q1, lam_k1, lam_q2, lam_k2,
              subln, w_oa, w_ob, w_out, norm_ffn, w_router_g, b_router_g, w_router_e, b_router_e,
              w1, w3, w2, norm_final):
    B, S, _ = x.shape
    slopes = alibi_slopes(HB)
    idx = split_indices(IN_WIDTHS)
    for l in range(DEPTH):
        h = rms_norm(x, norm_attn[l])
        proj = h @ w_in[l]
        c_q, c_kv, k_r, q_b, k_b, v_b, g_a, g_b = jnp.split(proj, idx, axis=-1)

        qa = (rms_norm(c_q, q_norm[l]) @ w_uq[l]).reshape(B, S, HA, QK_NOPE + QK_ROPE)
        q_rot = apply_rope(qa[..., QK_NOPE:], positions)
        kv = (rms_norm(c_kv, kv_norm[l]) @ w_ukv[l]).reshape(B, S, HA, QK_NOPE + V_DIM_A)
        k_nope, v_a = kv[..., :QK_NOPE], kv[..., QK_NOPE:]
        k_rot = jnp.broadcast_to(apply_rope(k_r[:, :, None, :], positions), (B, S, HA, QK_ROPE))
        q_a = jnp.concatenate([qa[..., :QK_NOPE], q_rot], axis=-1)
        k_a = jnp.concatenate([k_nope, k_rot], axis=-1)
        o_a = mla_attention(q_a, k_a, v_a).reshape(B, S, WIDTH_A)

        lam_init = 0.8 - 0.6 * math.exp(-0.3 * l)
        lam = (jnp.exp(jnp.sum(lam_q1[l].astype(jnp.float32) * lam_k1[l].astype(jnp.float32)))
               - jnp.exp(jnp.sum(lam_q2[l].astype(jnp.float32) * lam_k2[l].astype(jnp.float32))) + lam_init)
        o_b = diff_attention(q_b.reshape(B, S, HB, 2, DH_B), k_b.reshape(B, S, HB, 2, DH_B),
                             v_b.reshape(B, S, HB, 2 * DH_B), positions, lam, slopes)
        o_b = (rms_norm(o_b, subln[l]) * (1.0 - lam_init)).reshape(B, S, WIDTH_B)

        merged = jax.nn.sigmoid(g_a) * (o_a @ w_oa[l]) + jax.nn.sigmoid(g_b) * (o_b @ w_ob[l])
        x = x + merged @ w_out[l]

        h = rms_norm(x, norm_ffn[l])
        x = x + hier_moe(h, w_router_g[l], b_router_g[l], w_router_e[l], b_router_e[l], w1[l], w3[l], w2[l])
    return rms_norm(x, norm_final)
```

```python
import functools
import math

import jax
import jax.numpy as jnp
import numpy as np
from jax import lax
from jax.experimental import pallas as pl
from jax.experimental.pallas import tpu as pltpu

F32 = jnp.float32
BF16 = jnp.bfloat16
I32 = jnp.int32

EPS = 1e-6
ROPE_THETA = 10000.0
HA, QK_NOPE, QK_ROPE, V_DIM_A = 16, 128, 64, 128
Q_LORA, KV_LORA = 1024, 512
HB, DH_B = 8, 128
N_GROUPS, EXP_PER_GROUP, TOP_K = 8, 8, 2
N_EXPERTS = N_GROUPS * EXP_PER_GROUP
LOG2E = 1.4426950408889634
NEG = -1e30

LANES = 128
MOE_ROWS = 128
MIB = 1024 * 1024


def _cparams(semantics, vmem_mib):
    return pltpu.CompilerParams(dimension_semantics=semantics, vmem_limit_bytes=vmem_mib * MIB)


def _rms(xf, gain):
    ms = jnp.mean(xf * xf, axis=-1, keepdims=True)
    return xf * lax.rsqrt(ms + EPS) * gain


def _rmsnorm_body(x_ref, g_ref, o_ref):
    o_ref[...] = _rms(x_ref[...], g_ref[...]).astype(o_ref.dtype)


def _rmsnorm(x, gain, out_dtype, tm=256):
    s, d = x.shape
    tm = min(tm, s)
    return pl.pallas_call(
        _rmsnorm_body,
        grid=(s // tm,),
        in_specs=[pl.BlockSpec((tm, d), lambda i: (i, 0)), pl.BlockSpec((1, d), lambda i: (0, 0))],
        out_specs=pl.BlockSpec((tm, d), lambda i: (i, 0)),
        out_shape=jax.ShapeDtypeStruct((s, d), out_dtype),
        compiler_params=_cparams(("parallel",), 32),
        name="rmsnorm",
    )(x, gain.reshape(1, d))


def _matmul_body(*refs, n_extra, n_out, norm, epilogue):
    a_ref = refs[0]
    pos = 1
    if norm:
        g_ref = refs[pos]
        pos += 1
    b_ref = refs[pos]
    pos += 1
    extras = refs[pos:pos + n_extra]
    pos += n_extra
    outs = refs[pos:pos + n_out]
    pos += n_out
    if norm:
        lhs_ref = refs[pos]

        @pl.when(pl.program_id(1) == 0)
        def _():
            lhs_ref[...] = _rms(a_ref[...].astype(F32), g_ref[...]).astype(BF16)

        lhs = lhs_ref[...]
    else:
        lhs = a_ref[...]
    acc = jnp.dot(lhs, b_ref[...], preferred_element_type=F32)
    for o_ref, r in zip(outs, epilogue(acc, *extras)):
        o_ref[...] = r.astype(o_ref.dtype)


def _matmul(name, a, a_block, a_index, b, tn, outs, epilogue, gain=None, extras=(), vmem_mib=48):
    tm, k = a_block
    m = a.shape[0]
    n = b.shape[1]
    norm = gain is not None
    in_specs = [pl.BlockSpec(a_block, a_index)]
    args = [a]
    if norm:
        in_specs.append(pl.BlockSpec((1, k), lambda i, j: (0, 0)))
        args.append(gain.reshape(1, k).astype(F32))
    in_specs.append(pl.BlockSpec((k, tn), lambda i, j: (0, j)))
    args.append(b)
    for arr, blk, imap in extras:
        in_specs.append(pl.BlockSpec(blk, imap))
        args.append(arr)
    res = pl.pallas_call(
        functools.partial(_matmul_body, n_extra=len(extras), n_out=len(outs), norm=norm, epilogue=epilogue),
        grid=(m // tm, n // tn),
        in_specs=in_specs,
        out_specs=[pl.BlockSpec(blk, imap) for _, blk, imap in outs],
        out_shape=[sds for sds, _, _ in outs],
        scratch_shapes=[pltpu.VMEM((tm, k), BF16)] if norm else [],
        compiler_params=_cparams(("parallel", "arbitrary"), vmem_mib),
        name=name,
    )(*args)
    return res


def _rope_table_body(pos_ref, inv_ref, t_ref):
    ang = pos_ref[...].astype(F32) * inv_ref[...]
    c = jnp.cos(ang)
    s = jnp.sin(ang)
    lane = lax.broadcasted_iota(I32, ang.shape, 1)
    t_ref[...] = jnp.where(lane < 2 * (QK_ROPE // 2), c, jnp.where(lane < 3 * (QK_ROPE // 2), -s, s))


def _rope_table(positions_col, tm=512):
    s = positions_col.shape[0]
    tm = min(tm, s)
    half = QK_ROPE // 2
    inv = ROPE_THETA ** (-jnp.arange(half, dtype=F32) / half)
    inv4 = jnp.tile(inv, 4).reshape(1, LANES)
    return pl.pallas_call(
        _rope_table_body,
        grid=(s // tm,),
        in_specs=[pl.BlockSpec((tm, 1), lambda i: (i, 0)), pl.BlockSpec((1, LANES), lambda i: (0, 0))],
        out_specs=pl.BlockSpec((tm, LANES), lambda i: (i, 0)),
        out_shape=jax.ShapeDtypeStruct((s, LANES), F32),
        compiler_params=_cparams(("parallel",), 16),
        name="rope_table",
    )(positions_col, inv4)


def _rope_128(v, table):
    t = v * table
    return t + pltpu.roll(t, 2 * (QK_ROPE // 2), axis=1)


def _lane_tile(x, n):
    return x if n == 1 else jnp.concatenate([x] * n, axis=1)


def _softmax_step(s, v, m_ref, l_ref, acc_ref):
    tk = s.shape[1]
    dv = acc_ref.shape[1]
    m_prev = m_ref[...]
    m_new = jnp.maximum(m_prev, jnp.max(s, axis=1, keepdims=True))
    alpha = jnp.exp2(m_prev - m_new)
    p = jnp.exp2(s - _lane_tile(m_new, tk // LANES))
    l_ref[...] = alpha * l_ref[...] + jnp.sum(p, axis=1, keepdims=True)
    pv = jnp.dot(p.astype(BF16), v, preferred_element_type=F32)
    acc_ref[...] = acc_ref[...] * _lane_tile(alpha, dv // LANES) + pv
    m_ref[...] = m_new


def _qk(q, k):
    return lax.dot_general(q, k, (((1,), (1,)), ((), ())), preferred_element_type=F32)


def _causal_mask(s, row0, col0):
    row = row0 + lax.broadcasted_iota(I32, s.shape, 0)
    col = col0 + lax.broadcasted_iota(I32, s.shape, 1)
    return jnp.where(col <= row, s, NEG)


def _mla_body(q_ref, k_ref, v_ref, o_ref, m_ref, l_ref, acc_ref, *, tq, tk):
    i = pl.program_id(1)
    m_ref[...] = jnp.full(m_ref.shape, NEG, F32)
    l_ref[...] = jnp.zeros(l_ref.shape, F32)
    acc_ref[...] = jnp.zeros(acc_ref.shape, F32)
    q = q_ref[...]

    def chunk(start, masked):
        s = _qk(q, k_ref[pl.ds(start, tk), :])
        if masked:
            s = _causal_mask(s, i * tq, start)
        _softmax_step(s, v_ref[pl.ds(start, tk), :], m_ref, l_ref, acc_ref)

    def full_chunk(c, carry):
        chunk(pl.multiple_of(c * tk, tk), False)
        return carry

    lax.fori_loop(0, (i * tq) // tk, full_chunk, 0)
    for d in range(tq // tk):
        chunk(pl.multiple_of(i * tq + d * tk, tk), True)
    o_ref[...] = (acc_ref[...] / l_ref[...]).astype(o_ref.dtype)


def _mla_attention(q, k, v, tq=512, tk=512):
    h, s, dk = q.shape
    dv = v.shape[-1]
    tq, tk = min(tq, s), min(tk, s)
    return pl.pallas_call(
        functools.partial(_mla_body, tq=tq, tk=tk),
        grid=(h, s // tq),
        in_specs=[
            pl.BlockSpec((None, tq, dk), lambda hh, i: (hh, i, 0)),
            pl.BlockSpec((None, s, dk), lambda hh, i: (hh, 0, 0)),
            pl.BlockSpec((None, s, dv), lambda hh, i: (hh, 0, 0)),
        ],
        out_specs=pl.BlockSpec((tq, dv), lambda hh, i: (i, hh)),
        out_shape=jax.ShapeDtypeStruct((s, h * dv), BF16),
        scratch_shapes=[pltpu.VMEM((tq, LANES), F32), pltpu.VMEM((tq, LANES), F32), pltpu.VMEM((tq, dv), F32)],
        compiler_params=_cparams(("parallel", "arbitrary"), 48),
        name="mla_attention",
    )(q, k, v)


def _diff_body(slope_ref, q_ref, k_ref, v_ref, pcol_ref, prow_ref, lq1_ref, lk1_ref, lq2_ref, lk2_ref, sub_ref,
               o_ref, m0_ref, l0_ref, a0_ref, m1_ref, l1_ref, a1_ref, *, tq, tk, lam_init):
    h = pl.program_id(0)
    i = pl.program_id(1)
    for m_ref, l_ref, a_ref in ((m0_ref, l0_ref, a0_ref), (m1_ref, l1_ref, a1_ref)):
        m_ref[...] = jnp.full(m_ref.shape, NEG, F32)
        l_ref[...] = jnp.zeros(l_ref.shape, F32)
        a_ref[...] = jnp.zeros(a_ref.shape, F32)
    q0 = q_ref[:, :DH_B]
    q1 = q_ref[:, DH_B:]
    pq = pcol_ref[...]
    slope = slope_ref[h]

    def chunk(c, masked):
        start = pl.multiple_of(c * tk, tk)
        k = k_ref[pl.ds(start, tk), :]
        v = v_ref[pl.ds(start, tk), :]
        dist = jnp.abs(pq - prow_ref[c]).astype(F32)
        bias = dist * slope
        for qc, kc, refs in ((q0, k[:, :DH_B], (m0_ref, l0_ref, a0_ref)), (q1, k[:, DH_B:], (m1_ref, l1_ref, a1_ref))):
            s = _qk(qc, kc) - bias
            if masked:
                s = _causal_mask(s, i * tq, start)
            _softmax_step(s, v, *refs)

    def full_chunk(c, carry):
        chunk(c, False)
        return carry

    n_full = (i * tq) // tk
    lax.fori_loop(0, n_full, full_chunk, 0)
    for d in range(tq // tk):
        chunk(n_full + d, True)

    lam = (jnp.exp(jnp.sum(lq1_ref[...] * lk1_ref[...], axis=1, keepdims=True))
           - jnp.exp(jnp.sum(lq2_ref[...] * lk2_ref[...], axis=1, keepdims=True)) + lam_init)
    dv = a0_ref.shape[1]
    o0 = a0_ref[...] / _lane_tile(l0_ref[...], dv // LANES)
    o1 = a1_ref[...] / _lane_tile(l1_ref[...], dv // LANES)
    o = o0 - lam * o1
    o_ref[...] = (_rms(o, sub_ref[...]) * (1.0 - lam_init)).astype(o_ref.dtype)


def _diff_attention(qkv, pos_col, pos_row, slopes_l2, lam_vecs, subln, lam_init, tq=512, tk=512):
    _, s, dv = qkv.shape
    tq, tk = min(tq, s), min(tk, s)
    vec = pl.BlockSpec((1, DH_B), lambda hh, i: (0, 0))
    return pl.pallas_call(
        functools.partial(_diff_body, tq=tq, tk=tk, lam_init=lam_init),
        grid=(HB, s // tq),
        in_specs=[
            pl.BlockSpec(memory_space=pltpu.SMEM),
            pl.BlockSpec((None, tq, dv), lambda hh, i: (hh, i, 0)),
            pl.BlockSpec((None, s, dv), lambda hh, i: (HB + hh, 0, 0)),
            pl.BlockSpec((None, s, dv), lambda hh, i: (2 * HB + hh, 0, 0)),
            pl.BlockSpec((tq, 1), lambda hh, i: (i, 0)),
            pl.BlockSpec((s // tk, 1, tk), lambda hh, i: (0, 0, 0)),
            vec, vec, vec, vec,
            pl.BlockSpec((1, dv), lambda hh, i: (0, 0)),
        ],
        out_specs=pl.BlockSpec((tq, dv), lambda hh, i: (i, hh)),
        out_shape=jax.ShapeDtypeStruct((s, HB * dv), BF16),
        scratch_shapes=[pltpu.VMEM((tq, LANES), F32), pltpu.VMEM((tq, LANES), F32), pltpu.VMEM((tq, dv), F32)] * 2,
        compiler_params=_cparams(("parallel", "arbitrary"), 56),
        name="diff_attention",
    )(slopes_l2, qkv, qkv, qkv, pos_col, pos_row.reshape(s // tk, 1, tk), *lam_vecs, subln.reshape(1, dv))


def _merge_body(oa_ref, ob_ref, woa_ref, wob_ref, ga_ref, gb_ref, o_ref):
    ya = jnp.dot(oa_ref[...], woa_ref[...], preferred_element_type=F32)
    yb = jnp.dot(ob_ref[...], wob_ref[...], preferred_element_type=F32)
    out = jax.nn.sigmoid(ga_ref[...].astype(F32)) * ya + jax.nn.sigmoid(gb_ref[...].astype(F32)) * yb
    o_ref[...] = out.astype(o_ref.dtype)


def _merge(o_a, o_b, w_oa, w_ob, gates, tm=1024, tn=512):
    s, ka = o_a.shape
    kb = o_b.shape[1]
    d = w_oa.shape[1]
    tm = min(tm, s)
    nj = d // tn
    return pl.pallas_call(
        _merge_body,
        grid=(s // tm, nj),
        in_specs=[
            pl.BlockSpec((tm, ka), lambda i, j: (i, 0)),
            pl.BlockSpec((tm, kb), lambda i, j: (i, 0)),
            pl.BlockSpec((ka, tn), lambda i, j: (0, j)),
            pl.BlockSpec((kb, tn), lambda i, j: (0, j)),
            pl.BlockSpec((tm, tn), lambda i, j: (i, j)),
            pl.BlockSpec((tm, tn), lambda i, j: (i, nj + j)),
        ],
        out_specs=pl.BlockSpec((tm, tn), lambda i, j: (i, j)),
        out_shape=jax.ShapeDtypeStruct((s, d), BF16),
        compiler_params=_cparams(("parallel", "arbitrary"), 48),
        name="gated_merge",
    )(o_a, o_b, w_oa, w_ob, gates, gates)


def _router_body(x_ref, g_ref, whi_ref, wlo_ref, b_ref, ids_ref, gates_ref):
    h = _rms(x_ref[...], g_ref[...])
    h_hi = h.astype(BF16)
    h_lo = (h - h_hi.astype(F32)).astype(BF16)
    logits = (jnp.dot(h_hi, whi_ref[...], preferred_element_type=F32)
              + jnp.dot(h_lo, whi_ref[...], preferred_element_type=F32)
              + jnp.dot(h_hi, wlo_ref[...], preferred_element_type=F32)) + b_ref[...]
    lane = lax.broadcasted_iota(I32, logits.shape, 1).astype(F32)

    def first_argmax(vals, mx):
        return jnp.min(jnp.where(vals == mx, lane, 4.0 * LANES), axis=1, keepdims=True)

    is_g = jnp.logical_and(lane >= N_EXPERTS, lane < N_EXPERTS + N_GROUPS)
    lg = jnp.where(is_g, logits, NEG)
    mg = jnp.max(lg, axis=1, keepdims=True)
    p_g = 1.0 / jnp.sum(jnp.exp(lg - mg), axis=1, keepdims=True)
    g_idx = first_argmax(lg, mg) - N_EXPERTS
    lo = g_idx * EXP_PER_GROUP
    in_grp = jnp.logical_and(lane >= lo, lane < lo + EXP_PER_GROUP)
    le = jnp.where(in_grp, logits, NEG)
    m1 = jnp.max(le, axis=1, keepdims=True)
    i1 = first_argmax(le, m1)
    den = jnp.sum(jnp.exp(le - m1), axis=1, keepdims=True)
    le2 = jnp.where(lane == i1, NEG, le)
    m2 = jnp.max(le2, axis=1, keepdims=True)
    i2 = first_argmax(le2, m2)
    p1 = 1.0 / den
    p2 = jnp.exp(m2 - m1) / den
    psum = p1 + p2
    gate1 = p_g * p1 / psum
    gate2 = p_g * p2 / psum
    ids_ref[...] = jnp.where(lane == 0, i1, jnp.where(lane == 1, i2, 0.0)).astype(I32)
    gates_ref[...] = jnp.where(lane == 0, gate1, jnp.where(lane == 1, gate2, 0.0))


def _router(x, gain, w_hi, w_lo, bias, tm=256):
    t, d = x.shape
    tm = min(tm, t)
    return pl.pallas_call(
        _router_body,
        grid=(t // tm,),
        in_specs=[
            pl.BlockSpec((tm, d), lambda i: (i, 0)),
            pl.BlockSpec((1, d), lambda i: (0, 0)),
            pl.BlockSpec((d, LANES), lambda i: (0, 0)),
            pl.BlockSpec((d, LANES), lambda i: (0, 0)),
            pl.BlockSpec((1, LANES), lambda i: (0, 0)),
        ],
        out_specs=[pl.BlockSpec((tm, LANES), lambda i: (i, 0)), pl.BlockSpec((tm, LANES), lambda i: (i, 0))],
        out_shape=[jax.ShapeDtypeStruct((t, LANES), I32), jax.ShapeDtypeStruct((t, LANES), F32)],
        compiler_params=_cparams(("parallel",), 32),
        name="moe_router",
    )(x, gain.reshape(1, d), w_hi, w_lo, bias)


def _row_copies(src_hbm, dst_hbm, tok_ref, dst_ref, xbuf, ybuf, sem_in, sem_out):
    def gather(r):
        return pltpu.make_async_copy(src_hbm.at[pl.ds(tok_ref[0, r], 1)], xbuf.at[pl.ds(r, 1)], sem_in)

    def scatter(r):
        return pltpu.make_async_copy(ybuf.at[pl.ds(r, 1)], dst_hbm.at[pl.ds(dst_ref[0, r], 1)], sem_out)

    return gather, scatter


def _expert_body(blk_exp_ref, nb_ref, tok_ref, dst_ref, x_hbm, g_ref, w_ref, w1_ref, w3_ref, w2_ref, z_hbm,
                 xbuf, ybuf, sem_in, sem_out):
    del blk_exp_ref
    b = pl.program_id(0)
    rows = xbuf.shape[0]
    gather, scatter = _row_copies(x_hbm, z_hbm, tok_ref, dst_ref, xbuf, ybuf, sem_in, sem_out)

    @pl.when(b < nb_ref[0])
    def _():
        def start_in(r, c):
            gather(r).start()
            return c

        def wait_in(r, c):
            gather(r).wait()
            return c

        lax.fori_loop(0, rows, start_in, 0)
        lax.fori_loop(0, rows, wait_in, 0)
        xn = _rms(xbuf[...], g_ref[...]).astype(BF16)
        a1 = jnp.dot(xn, w1_ref[...], preferred_element_type=F32)
        a3 = jnp.dot(xn, w3_ref[...], preferred_element_type=F32)
        hmid = (jax.nn.silu(a1) * a3).astype(BF16)
        y = jnp.dot(hmid, w2_ref[...], preferred_element_type=F32)
        ybuf[...] = y * w_ref[...]

        def start_out(r, c):
            scatter(r).start()
            return c

        def wait_out(r, c):
            scatter(r).wait()
            return c

        lax.fori_loop(0, rows, start_out, 0)
        lax.fori_loop(0, rows, wait_out, 0)

    @pl.when(b >= nb_ref[0])
    def _():
        ybuf[...] = jnp.zeros(ybuf.shape, F32)
        fill = pltpu.make_async_copy(ybuf, z_hbm.at[pl.ds(pl.multiple_of(b * rows, rows), rows)], sem_out)
        fill.start()
        fill.wait()


def _experts(x, gain, blk_exp, nb_used, row_tok, row_dst, row_w, w1, w3, w2):
    t, d = x.shape
    ne, _, de = w1.shape
    nb = blk_exp.shape[0]
    rows = MOE_ROWS
    grid_spec = pltpu.PrefetchScalarGridSpec(
        num_scalar_prefetch=2,
        grid=(nb,),
        in_specs=[
            pl.BlockSpec((None, 1, rows), lambda b, be, nu: (b, 0, 0), memory_space=pltpu.SMEM),
            pl.BlockSpec((None, 1, rows), lambda b, be, nu: (b, 0, 0), memory_space=pltpu.SMEM),
            pl.BlockSpec(memory_space=pl.ANY),
            pl.BlockSpec((1, d), lambda b, be, nu: (0, 0)),
            pl.BlockSpec((rows, 1), lambda b, be, nu: (b, 0)),
            pl.BlockSpec((None, d, de), lambda b, be, nu: (be[b], 0, 0)),
            pl.BlockSpec((None, d, de), lambda b, be, nu: (be[b], 0, 0)),
            pl.BlockSpec((None, de, d), lambda b, be, nu: (be[b], 0, 0)),
        ],
        out_specs=pl.BlockSpec(memory_space=pl.ANY),
        scratch_shapes=[pltpu.VMEM((rows, d), F32), pltpu.VMEM((rows, d), F32),
                        pltpu.SemaphoreType.DMA(()), pltpu.SemaphoreType.DMA(())],
    )
    return pl.pallas_call(
        _expert_body,
        grid_spec=grid_spec,
        out_shape=jax.ShapeDtypeStruct((nb * rows, d), F32),
        compiler_params=_cparams(("arbitrary",), 48),
        name="moe_experts",
    )(blk_exp, nb_used, row_tok.reshape(nb, 1, rows), row_dst.reshape(nb, 1, rows), x, gain.reshape(1, d),
      row_w.reshape(nb * rows, 1), w1, w3, w2)


def _dispatch_plan(ids, gates, t):
    a = t * TOP_K
    rows = MOE_ROWS
    nb = a // rows + N_EXPERTS
    p = nb * rows
    e = ids[:, :TOP_K].reshape(a)
    gate = gates[:, :TOP_K].reshape(a)
    order = jnp.argsort(e, stable=True).astype(I32)
    se = e[order]
    experts = jnp.arange(N_EXPERTS, dtype=I32)
    seg_start = jnp.searchsorted(se, experts, side="left").astype(I32)
    seg_end = jnp.searchsorted(se, experts, side="right").astype(I32)
    counts = seg_end - seg_start
    padded = (counts + rows - 1) // rows * rows
    pad_end = jnp.cumsum(padded).astype(I32)
    pad_start = pad_end - padded
    nb_used = (pad_end[-1] // rows).reshape(1)
    blk_exp = jnp.clip(jnp.searchsorted(pad_end, jnp.arange(nb, dtype=I32) * rows, side="right"),
                       0, N_EXPERTS - 1).astype(I32)
    r = jnp.arange(p, dtype=I32)
    er = blk_exp[r // rows]
    off = r - pad_start[er]
    valid = jnp.logical_and(r < pad_end[-1], off < counts[er])
    n_valid_before = jnp.where(r < pad_end[-1], seg_start[er] + jnp.minimum(off, counts[er]), a)
    arow = order[jnp.clip(seg_start[er] + off, 0, a - 1)]
    row_tok = jnp.where(valid, arow // TOP_K, 0).astype(I32)
    row_w = jnp.where(valid, gate[arow], 0.0).astype(F32)
    row_dst = jnp.where(valid, arow, a + (r - n_valid_before)).astype(I32)
    return blk_exp, nb_used.astype(I32), row_tok, row_dst, row_w


def _combine_body(x_ref, z_ref, g_ref, o_ref, *, final):
    d = x_ref.shape[1]
    x = x_ref[...] + (z_ref[:, :d] + z_ref[:, d:])
    o_ref[...] = _rms(x, g_ref[...]) if final else x


def _combine(x, z, gain, final, tm=256):
    t, d = x.shape
    tm = min(tm, t)
    z2 = z.reshape(z.shape[0] // TOP_K, TOP_K * d)
    return pl.pallas_call(
        functools.partial(_combine_body, final=final),
        grid=(t // tm,),
        in_specs=[pl.BlockSpec((tm, d), lambda i: (i, 0)), pl.BlockSpec((tm, TOP_K * d), lambda i: (i, 0)),
                  pl.BlockSpec((1, d), lambda i: (0, 0))],
        out_specs=pl.BlockSpec((tm, d), lambda i: (i, 0)),
        out_shape=jax.ShapeDtypeStruct((t, d), F32),
        compiler_params=_cparams(("parallel",), 48),
        name="moe_combine",
    )(x, z2, gain.reshape(1, d))


def _rope_cols(w_rope):
    half = QK_ROPE // 2
    x1, x2 = w_rope[..., :half], w_rope[..., half:]
    return jnp.concatenate([x1, x2, x2, x1], axis=-1)


def _layer_weights(w_in, w_uq, w_ukv, w_router_g, b_router_g, w_router_e, b_router_e):
    d = w_in.shape[0]
    o = np.cumsum([0, Q_LORA, KV_LORA, QK_ROPE, HB * 2 * DH_B, HB * 2 * DH_B, HB * 2 * DH_B, d, d])
    w_cq, w_ckv, w_kr, w_qb, w_kb, w_vb, w_ga, w_gb = (w_in[:, o[n]:o[n + 1]] for n in range(8))
    pad = jnp.zeros((d, 2048 - Q_LORA - KV_LORA - LANES), w_in.dtype)
    w_lat = jnp.concatenate([w_cq, w_ckv, _rope_cols(w_kr), pad], axis=1).astype(BF16)
    w_qkvb = jnp.concatenate([w_qb, w_kb, w_vb], axis=1).astype(BF16)
    w_gate = jnp.concatenate([w_ga, w_gb], axis=1).astype(BF16)
    wq = w_uq.reshape(Q_LORA, HA, QK_NOPE + QK_ROPE)
    wq = jnp.concatenate([wq[..., :QK_NOPE], _rope_cols(wq[..., QK_NOPE:])], axis=-1).reshape(Q_LORA, HA * 2 * LANES)
    w_r = jnp.concatenate([w_router_e, w_router_g, jnp.zeros((d, LANES - N_EXPERTS - N_GROUPS), F32)], axis=1)
    w_r_hi = w_r.astype(BF16)
    w_r_lo = (w_r - w_r_hi.astype(F32)).astype(BF16)
    b_r = jnp.concatenate([b_router_e, b_router_g, jnp.zeros((LANES - N_EXPERTS - N_GROUPS,), F32)]).reshape(1, LANES)
    return w_lat, w_qkvb, w_gate, wq.astype(BF16), w_ukv.astype(BF16), w_r_hi, w_r_lo, b_r


def kernel(x, positions, norm_attn, w_in, q_norm, w_uq, kv_norm, w_ukv, lam_q1, lam_k1, lam_q2, lam_k2, subln, w_oa,
           w_ob, w_out, norm_ffn, w_router_g, b_router_g, w_router_e, b_router_e, w1, w3, w2, norm_final):
    bsz, s, d = x.shape
    assert bsz == 1
    depth = w_in.shape[0]
    x = x.reshape(s, d)
    pos_col = positions.reshape(s, 1)
    pos_row = positions.reshape(1, s)
    table = _rope_table(pos_col)
    slopes_l2 = jnp.asarray(2.0 ** (-8.0 * (np.arange(HB) + 1) / HB) * LOG2E, dtype=F32)
    c_a = (QK_NOPE + QK_ROPE) ** -0.5 * LOG2E
    c_b = DH_B ** -0.5 * LOG2E
    tm = min(1024, s)
    hd = 2 * LANES

    for l in range(depth):
        w_lat, w_qkvb, w_gate, wq, wkv, w_r_hi, w_r_lo, b_r = _layer_weights(
            w_in[l], w_uq[l], w_ukv[l], w_router_g[l], b_router_g[l], w_router_e[l], b_router_e[l])
        h = _rmsnorm(x, norm_attn[l], BF16)
        full = ((tm, d), lambda i, j: (i, 0))

        def ident(acc):
            return (acc,)

        (lat,) = _matmul("in_proj_latent", h, *full, w_lat, 512,
                         [(jax.ShapeDtypeStruct((s, 2048), BF16), (tm, 512), lambda i, j: (i, j))], ident)

        def scale_q(acc):
            return (acc * jnp.where(pl.program_id(1) < HB, c_b, 1.0),)

        (qkvb,) = _matmul("in_proj_mixer_b", h, *full, w_qkvb, hd,
                          [(jax.ShapeDtypeStruct((3 * HB, s, hd), BF16), (None, tm, hd), lambda i, j: (j, i, 0))],
                          scale_q)
        (gates,) = _matmul("in_proj_gates", h, *full, w_gate, 512,
                           [(jax.ShapeDtypeStruct((s, 2 * d), BF16), (tm, 512), lambda i, j: (i, j))], ident)

        def q_epilogue(acc, t_ref):
            rot = _rope_128(acc[:, LANES:], t_ref[...])
            return (jnp.concatenate([acc[:, :LANES], rot], axis=1) * c_a,)

        (q_a,) = _matmul("q_up_proj", lat, (tm, Q_LORA), lambda i, j: (i, 0), wq, hd,
                         [(jax.ShapeDtypeStruct((HA, s, hd), BF16), (None, tm, hd), lambda i, j: (j, i, 0))],
                         q_epilogue, gain=q_norm[l], extras=[(table, (tm, LANES), lambda i, j: (i, 0))])

        def kv_epilogue(acc, kr_ref, t_ref):
            rot = _rope_128(kr_ref[...].astype(F32), t_ref[...])
            lane = lax.broadcasted_iota(I32, rot.shape, 1)
            rot = jnp.where(lane < QK_ROPE, rot, 0.0)
            return jnp.concatenate([acc[:, :LANES], rot], axis=1), acc[:, LANES:]

        k_a, v_a = _matmul("kv_up_proj", lat, (tm, KV_LORA), lambda i, j: (i, Q_LORA // KV_LORA), wkv, hd,
                           [(jax.ShapeDtypeStruct((HA, s, hd), BF16), (None, tm, hd), lambda i, j: (j, i, 0)),
                            (jax.ShapeDtypeStruct((HA, s, V_DIM_A), BF16), (None, tm, V_DIM_A),
                             lambda i, j: (j, i, 0))],
                           kv_epilogue, gain=kv_norm[l],
                           extras=[(lat, (tm, LANES), lambda i, j: (i, (Q_LORA + KV_LORA) // LANES)),
                                   (table, (tm, LANES), lambda i, j: (i, 0))])
        o_a = _mla_attention(q_a, k_a, v_a)

        lam_init = 0.8 - 0.6 * math.exp(-0.3 * l)
        lam_vecs = [v[l].reshape(1, DH_B).astype(F32) for v in (lam_q1, lam_k1, lam_q2, lam_k2)]
        o_b = _diff_attention(qkvb, pos_col, pos_row, slopes_l2, lam_vecs, subln[l], lam_init)

        merged = _merge(o_a, o_b, w_oa[l].astype(BF16), w_ob[l].astype(BF16), gates)

        def residual(acc, x_ref):
            return (x_ref[...] + acc,)

        (x,) = _matmul("out_proj", merged, (tm, d), lambda i, j: (i, 0), w_out[l].astype(BF16), 512,
                       [(jax.ShapeDtypeStruct((s, d), F32), (tm, 512), lambda i, j: (i, j))], residual,
                       extras=[(x, (tm, 512), lambda i, j: (i, j))])

        ids, gate_vals = _router(x, norm_ffn[l], w_r_hi, w_r_lo, b_r)
        blk_exp, nb_used, row_tok, row_dst, row_w = _dispatch_plan(ids, gate_vals, s)
        z = _experts(x, norm_ffn[l], blk_exp, nb_used, row_tok, row_dst, row_w,
                     w1[l].astype(BF16), w3[l].astype(BF16), w2[l].astype(BF16))
        final = l == depth - 1
        x = _combine(x, z, norm_final if final else norm_ffn[l], final)
    return x.reshape(bsz, s, d)
```

```python
import functools
import math

import jax
import jax.numpy as jnp
import numpy as np
from jax import lax
from jax.experimental import pallas as pl
from jax.experimental.pallas import tpu as pltpu

F32 = jnp.float32
BF16 = jnp.bfloat16
I32 = jnp.int32
U32 = jnp.uint32

EPS = 1e-6
ROPE_THETA = 10000.0
HA, QK_NOPE, QK_ROPE, V_DIM_A = 16, 128, 64, 128
Q_LORA, KV_LORA = 1024, 512
HB, DH_B = 8, 128
N_GROUPS, EXP_PER_GROUP, TOP_K = 8, 8, 2
N_EXPERTS = N_GROUPS * EXP_PER_GROUP
LOG2E = 1.4426950408889634
NEG = -1e30

LANES = 128
MOE_ROWS = 128
ATT_T = 512
ATT_UNROLL = 4
ONES_ROWS = 16
MIB = 1024 * 1024


def _cparams(semantics, vmem_mib):
    return pltpu.CompilerParams(dimension_semantics=semantics, vmem_limit_bytes=vmem_mib * MIB)


def _rms(xf, gain):
    ms = jnp.mean(xf * xf, axis=-1, keepdims=True)
    return xf * lax.rsqrt(ms + EPS) * gain


def _rmsnorm_body(x_ref, g_ref, o_ref):
    o_ref[...] = _rms(x_ref[...], g_ref[...]).astype(o_ref.dtype)


def _rmsnorm(x, gain, out_dtype, tm=256):
    s, d = x.shape
    tm = min(tm, s)
    return pl.pallas_call(
        _rmsnorm_body,
        grid=(s // tm,),
        in_specs=[pl.BlockSpec((tm, d), lambda i: (i, 0)), pl.BlockSpec((1, d), lambda i: (0, 0))],
        out_specs=pl.BlockSpec((tm, d), lambda i: (i, 0)),
        out_shape=jax.ShapeDtypeStruct((s, d), out_dtype),
        compiler_params=_cparams(("parallel",), 32),
        name="rmsnorm",
    )(x, gain.reshape(1, d))


def _matmul_body(*refs, n_extra, n_out, norm, epilogue):
    a_ref = refs[0]
    pos = 1
    if norm:
        g_ref = refs[pos]
        pos += 1
    b_ref = refs[pos]
    pos += 1
    extras = refs[pos:pos + n_extra]
    pos += n_extra
    outs = refs[pos:pos + n_out]
    pos += n_out
    if norm:
        lhs_ref = refs[pos]

        @pl.when(pl.program_id(1) == 0)
        def _():
            lhs_ref[...] = _rms(a_ref[...].astype(F32), g_ref[...]).astype(BF16)

        lhs = lhs_ref[...]
    else:
        lhs = a_ref[...]
    acc = jnp.dot(lhs, b_ref[...], preferred_element_type=F32)
    for o_ref, r in zip(outs, epilogue(acc, *extras)):
        o_ref[...] = r.astype(o_ref.dtype)


def _matmul(name, a, a_block, a_index, b, tn, outs, epilogue, gain=None, extras=(), vmem_mib=48):
    tm, k = a_block
    m = a.shape[0]
    n = b.shape[1]
    norm = gain is not None
    in_specs = [pl.BlockSpec(a_block, a_index)]
    args = [a]
    if norm:
        in_specs.append(pl.BlockSpec((1, k), lambda i, j: (0, 0)))
        args.append(gain.reshape(1, k).astype(F32))
    in_specs.append(pl.BlockSpec((k, tn), lambda i, j: (0, j)))
    args.append(b)
    for arr, blk, imap in extras:
        in_specs.append(pl.BlockSpec(blk, imap))
        args.append(arr)
    res = pl.pallas_call(
        functools.partial(_matmul_body, n_extra=len(extras), n_out=len(outs), norm=norm, epilogue=epilogue),
        grid=(m // tm, n // tn),
        in_specs=in_specs,
        out_specs=[pl.BlockSpec(blk, imap) for _, blk, imap in outs],
        out_shape=[sds for sds, _, _ in outs],
        scratch_shapes=[pltpu.VMEM((tm, k), BF16)] if norm else [],
        compiler_params=_cparams(("parallel", "arbitrary"), vmem_mib),
        name=name,
    )(*args)
    return res


def _rope_table_body(pos_ref, inv_ref, t_ref):
    ang = pos_ref[...].astype(F32) * inv_ref[...]
    c = jnp.cos(ang)
    s = jnp.sin(ang)
    lane = lax.broadcasted_iota(I32, ang.shape, 1)
    t_ref[...] = jnp.where(lane < 2 * (QK_ROPE // 2), c, jnp.where(lane < 3 * (QK_ROPE // 2), -s, s))


def _rope_table(positions_col, tm=512):
    s = positions_col.shape[0]
    tm = min(tm, s)
    half = QK_ROPE // 2
    inv = ROPE_THETA ** (-jnp.arange(half, dtype=F32) / half)
    inv4 = jnp.tile(inv, 4).reshape(1, LANES)
    return pl.pallas_call(
        _rope_table_body,
        grid=(s // tm,),
        in_specs=[pl.BlockSpec((tm, 1), lambda i: (i, 0)), pl.BlockSpec((1, LANES), lambda i: (0, 0))],
        out_specs=pl.BlockSpec((tm, LANES), lambda i: (i, 0)),
        out_shape=jax.ShapeDtypeStruct((s, LANES), F32),
        compiler_params=_cparams(("parallel",), 16),
        name="rope_table",
    )(positions_col, inv4)


def _rope_128(v, table):
    t = v * table
    return t + pltpu.roll(t, 2 * (QK_ROPE // 2), axis=1)


def _softmax_step_t(st, vt, m_ref, acc_ref):
    m_prev = m_ref[...]
    m_new = jnp.maximum(m_prev, jnp.max(st, axis=0, keepdims=True))
    alpha = jnp.exp2(m_prev - m_new)
    p = jnp.exp2(st - m_new)
    acc_ref[...] = acc_ref[...] * alpha + jnp.dot(vt, p.astype(BF16), preferred_element_type=F32)
    m_ref[...] = m_new


def _kq(k, q):
    return lax.dot_general(k, q, (((1,), (1,)), ((), ())), preferred_element_type=F32)


def _causal_mask_t(st, key0, query0):
    key = key0 + lax.broadcasted_iota(I32, st.shape, 0)
    qry = query0 + lax.broadcasted_iota(I32, st.shape, 1)
    return jnp.where(key <= qry, st, NEG)


def _init_stats(m_ref, acc_ref):
    m_ref[...] = jnp.full(m_ref.shape, NEG, F32)
    acc_ref[...] = jnp.zeros(acc_ref.shape, F32)


def _normalized(acc_ref, chain, dv):
    return acc_ref[chain, :dv, :] / acc_ref[chain, dv:dv + 1, :]


def _causal_pipeline(n_common, chains, scores, update, s_ref, unroll=ATT_UNROLL):
    assert unroll % 2 == 0

    def ahead(c, slot):
        for a, st in zip(chains, scores(c)):
            s_ref[2 * a + slot] = st

    def fold(c, slot, masked=False):
        for a in chains:
            update(c, a, s_ref[2 * a + slot], masked)

    def step(c, t):
        ahead(c + 1, (t + 1) % 2)
        fold(c, t % 2)

    ahead(0, 0)
    n_groups = n_common // unroll

    def body(g, carry):
        for t in range(unroll):
            step(g * unroll + t, t)
        return carry

    lax.fori_loop(0, n_groups, body, 0)
    c0 = n_groups * unroll
    rem = n_common - c0
    for t in range(unroll - 1):
        @pl.when(t < rem)
        def _(t=t):
            step(c0 + t, t)

    for parity in range(2):
        @pl.when(rem % 2 == parity)
        def _(parity=parity):
            fold(n_common, parity, True)


def _mla_body(q_ref, k_ref, vt_ref, o_ref, m_ref, acc_ref, s_ref, *, tq):
    i = pl.program_id(1)
    _init_stats(m_ref, acc_ref)
    q = q_ref[...]

    def scores(c):
        return [_kq(k_ref[pl.ds(pl.multiple_of(c * tq, tq), tq), :], q)]

    def update(c, a, st, masked):
        if masked:
            st = _causal_mask_t(st, c * tq, i * tq)
        _softmax_step_t(st, vt_ref[c], m_ref.at[a], acc_ref.at[a])

    _causal_pipeline(i, [0], scores, update, s_ref)
    o_ref[...] = _normalized(acc_ref, 0, o_ref.shape[1]).T.astype(o_ref.dtype)


def _mla_attention(q, k, vt):
    h, s, dk = q.shape
    _, nchunk, dvp, tq = vt.shape
    dv = dvp - ONES_ROWS
    return pl.pallas_call(
        functools.partial(_mla_body, tq=tq),
        grid=(h, nchunk),
        in_specs=[
            pl.BlockSpec((None, tq, dk), lambda hh, i: (hh, i, 0)),
            pl.BlockSpec((None, s, dk), lambda hh, i: (hh, 0, 0)),
            pl.BlockSpec((None, nchunk, dvp, tq), lambda hh, i: (hh, 0, 0, 0)),
        ],
        out_specs=pl.BlockSpec((tq, dv), lambda hh, i: (i, hh)),
        out_shape=jax.ShapeDtypeStruct((s, h * dv), BF16),
        scratch_shapes=[pltpu.VMEM((1, 1, tq), F32), pltpu.VMEM((1, dvp, tq), F32), pltpu.VMEM((2, tq, tq), F32)],
        compiler_params=_cparams(("parallel", "arbitrary"), 48),
        name="mla_attention",
    )(q, k, vt)


def _diff_body(slope_ref, q_ref, k_ref, vt_ref, pcol_ref, prow_ref, lq1_ref, lk1_ref, lq2_ref, lk2_ref, sub_ref,
               o_ref, m_ref, acc_ref, s_ref, *, tq, lam_init):
    h = pl.program_id(0)
    i = pl.program_id(1)
    _init_stats(m_ref, acc_ref)
    qs = [q_ref[:, c * DH_B:(c + 1) * DH_B] for c in range(2)]
    pq = prow_ref[...]
    slope = slope_ref[h]

    def scores(c):
        start = pl.multiple_of(c * tq, tq)
        k = k_ref[pl.ds(start, tq), :]
        bias = jnp.abs(pcol_ref[pl.ds(start, tq), :] - pq) * slope
        return [_kq(k[:, mp * DH_B:(mp + 1) * DH_B], qs[mp]) - bias for mp in range(2)]

    def update(c, chain, st, masked):
        if masked:
            st = _causal_mask_t(st, c * tq, i * tq)
        _softmax_step_t(st, vt_ref[c], m_ref.at[chain], acc_ref.at[chain])

    _causal_pipeline(i, [0, 1], scores, update, s_ref)

    lam = (jnp.exp(jnp.sum(lq1_ref[...] * lk1_ref[...], axis=1, keepdims=True))
           - jnp.exp(jnp.sum(lq2_ref[...] * lk2_ref[...], axis=1, keepdims=True)) + lam_init)
    dv = o_ref.shape[1]
    o = _normalized(acc_ref, 0, dv) - lam * _normalized(acc_ref, 1, dv)
    ms = jnp.mean(o * o, axis=0, keepdims=True)
    y = o * lax.rsqrt(ms + EPS) * sub_ref[...] * (1.0 - lam_init)
    o_ref[...] = y.T.astype(o_ref.dtype)


def _diff_attention(qk, vt, pos_col, pos_row, slopes_l2, lam_vecs, subln, lam_init):
    _, s, dqk = qk.shape
    _, nchunk, dvp, tq = vt.shape
    dv = dvp - ONES_ROWS
    vec = pl.BlockSpec((1, DH_B), lambda hh, i: (0, 0))
    once = pl.Buffered(1)
    return pl.pallas_call(
        functools.partial(_diff_body, tq=tq, lam_init=lam_init),
        grid=(HB, nchunk),
        in_specs=[
            pl.BlockSpec(memory_space=pltpu.SMEM),
            pl.BlockSpec((None, tq, dqk), lambda hh, i: (hh, i, 0)),
            pl.BlockSpec((None, s, dqk), lambda hh, i: (HB + hh, 0, 0), pipeline_mode=once),
            pl.BlockSpec((None, nchunk, dvp, tq), lambda hh, i: (hh, 0, 0, 0), pipeline_mode=once),
            pl.BlockSpec((s, 1), lambda hh, i: (0, 0), pipeline_mode=once),
            pl.BlockSpec((1, tq), lambda hh, i: (0, i)),
            vec, vec, vec, vec,
            pl.BlockSpec((dv, 1), lambda hh, i: (0, 0)),
        ],
        out_specs=pl.BlockSpec((tq, dv), lambda hh, i: (i, hh)),
        out_shape=jax.ShapeDtypeStruct((s, HB * dv), BF16),
        scratch_shapes=[pltpu.VMEM((2, 1, tq), F32), pltpu.VMEM((2, dvp, tq), F32), pltpu.VMEM((4, tq, tq), F32)],
        compiler_params=_cparams(("parallel", "arbitrary"), 56),
        name="diff_attention",
    )(slopes_l2, qk, qk, vt, pos_col, pos_row, *lam_vecs, subln.reshape(dv, 1))


def _merge_body(oa_ref, ob_ref, woa_ref, wob_ref, ga_ref, gb_ref, o_ref):
    ya = jnp.dot(oa_ref[...], woa_ref[...], preferred_element_type=F32)
    yb = jnp.dot(ob_ref[...], wob_ref[...], preferred_element_type=F32)
    out = jax.nn.sigmoid(ga_ref[...].astype(F32)) * ya + jax.nn.sigmoid(gb_ref[...].astype(F32)) * yb
    o_ref[...] = out.astype(o_ref.dtype)


def _merge(o_a, o_b, w_oa, w_ob, gates, tm=1024, tn=512):
    s, ka = o_a.shape
    kb = o_b.shape[1]
    d = w_oa.shape[1]
    tm = min(tm, s)
    nj = d // tn
    return pl.pallas_call(
        _merge_body,
        grid=(s // tm, nj),
        in_specs=[
            pl.BlockSpec((tm, ka), lambda i, j: (i, 0)),
            pl.BlockSpec((tm, kb), lambda i, j: (i, 0)),
            pl.BlockSpec((ka, tn), lambda i, j: (0, j)),
            pl.BlockSpec((kb, tn), lambda i, j: (0, j)),
            pl.BlockSpec((tm, tn), lambda i, j: (i, j)),
            pl.BlockSpec((tm, tn), lambda i, j: (i, nj + j)),
        ],
        out_specs=pl.BlockSpec((tm, tn), lambda i, j: (i, j)),
        out_shape=jax.ShapeDtypeStruct((s, d), BF16),
        compiler_params=_cparams(("parallel", "arbitrary"), 48),
        name="gated_merge",
    )(o_a, o_b, w_oa, w_ob, gates, gates)


def _pack_pairs(x):
    half = x.shape[1] // 2
    lo = pltpu.bitcast(x[:, :half].astype(BF16).astype(F32), U32)
    hi = pltpu.bitcast(x[:, half:].astype(BF16).astype(F32), U32)
    return (lo >> 16) | (hi & jnp.uint32(0xFFFF0000))


def _unpack_pairs(p):
    lo = pltpu.bitcast(p << 16, F32)
    hi = pltpu.bitcast(p & jnp.uint32(0xFFFF0000), F32)
    return jnp.concatenate([lo, hi], axis=1)


def _router_body(x_ref, g_ref, whi_ref, wlo_ref, b_ref, tri_ref, ids_ref, gates_ref, hp_ref, cnt_ref):
    @pl.when(pl.program_id(0) == 0)
    def _():
        cnt_ref[...] = jnp.zeros(cnt_ref.shape, F32)

    h = _rms(x_ref[...], g_ref[...])
    hp_ref[...] = _pack_pairs(h)
    h_hi = h.astype(BF16)
    h_lo = (h - h_hi.astype(F32)).astype(BF16)
    logits = (jnp.dot(h_hi, whi_ref[...], preferred_element_type=F32)
              + jnp.dot(h_lo, whi_ref[...], preferred_element_type=F32)
              + jnp.dot(h_hi, wlo_ref[...], preferred_element_type=F32)) + b_ref[...]
    lane = lax.broadcasted_iota(I32, logits.shape, 1).astype(F32)

    def first_argmax(vals, mx):
        return jnp.min(jnp.where(vals == mx, lane, 4.0 * LANES), axis=1, keepdims=True)

    is_g = jnp.logical_and(lane >= N_EXPERTS, lane < N_EXPERTS + N_GROUPS)
    lg = jnp.where(is_g, logits, NEG)
    mg = jnp.max(lg, axis=1, keepdims=True)
    p_g = 1.0 / jnp.sum(jnp.exp(lg - mg), axis=1, keepdims=True)
    g_idx = first_argmax(lg, mg) - N_EXPERTS
    lo = g_idx * EXP_PER_GROUP
    in_grp = jnp.logical_and(lane >= lo, lane < lo + EXP_PER_GROUP)
    le = jnp.where(in_grp, logits, NEG)
    m1 = jnp.max(le, axis=1, keepdims=True)
    i1 = first_argmax(le, m1)
    den = jnp.sum(jnp.exp(le - m1), axis=1, keepdims=True)
    le2 = jnp.where(lane == i1, NEG, le)
    m2 = jnp.max(le2, axis=1, keepdims=True)
    i2 = first_argmax(le2, m2)
    p1 = 1.0 / den
    p2 = jnp.exp(m2 - m1) / den
    psum = p1 + p2
    gate1 = p_g * p1 / psum
    gate2 = p_g * p2 / psum

    hot1 = jnp.where(lane == i1, 1.0, 0.0)
    hot2 = jnp.where(lane == i2, 1.0, 0.0)
    tri = tri_ref[...]
    before1 = jnp.dot(tri, hot1.astype(BF16), preferred_element_type=F32) + cnt_ref[...]
    tot1 = jnp.sum(hot1, axis=0, keepdims=True)
    before2 = jnp.dot(tri, hot2.astype(BF16), preferred_element_type=F32) + (cnt_ref[...] + tot1)
    r1 = jnp.sum(hot1 * before1, axis=1, keepdims=True)
    r2 = jnp.sum(hot2 * before2, axis=1, keepdims=True)
    cnt_ref[...] = cnt_ref[...] + tot1 + jnp.sum(hot2, axis=0, keepdims=True)

    ids = jnp.where(lane == 0, i1, jnp.where(lane == 1, i2, jnp.where(lane == 2, r1, jnp.where(lane == 3, r2, 0.0))))
    ids_ref[...] = ids.astype(I32)
    gates_ref[...] = jnp.where(lane == 0, gate1, jnp.where(lane == 1, gate2, 0.0))


def _router(x, gain, w_hi, w_lo, bias, tm=256):
    t, d = x.shape
    tm = min(tm, t)
    tri = jnp.tril(jnp.ones((tm, tm), F32), -1).astype(BF16)
    return pl.pallas_call(
        _router_body,
        grid=(t // tm,),
        in_specs=[
            pl.BlockSpec((tm, d), lambda i: (i, 0)),
            pl.BlockSpec((1, d), lambda i: (0, 0)),
            pl.BlockSpec((d, LANES), lambda i: (0, 0)),
            pl.BlockSpec((d, LANES), lambda i: (0, 0)),
            pl.BlockSpec((1, LANES), lambda i: (0, 0)),
            pl.BlockSpec((tm, tm), lambda i: (0, 0)),
        ],
        out_specs=[pl.BlockSpec((tm, LANES), lambda i: (i, 0)), pl.BlockSpec((tm, LANES), lambda i: (i, 0)),
                   pl.BlockSpec((tm, d // 2), lambda i: (i, 0)), pl.BlockSpec((1, LANES), lambda i: (0, 0))],
        out_shape=[jax.ShapeDtypeStruct((t, LANES), I32), jax.ShapeDtypeStruct((t, LANES), F32),
                   jax.ShapeDtypeStruct((t, d // 2), U32), jax.ShapeDtypeStruct((1, LANES), F32)],
        compiler_params=_cparams(("arbitrary",), 40),
        name="moe_router",
    )(x, gain.reshape(1, d), w_hi, w_lo, bias, tri)


def _dispatch_plan(ids, counts, t):
    rows = MOE_ROWS
    nb = t * TOP_K // rows + N_EXPERTS
    e = ids[:, :TOP_K]
    rank = ids[:, TOP_K:2 * TOP_K]
    cnt = counts[0, :N_EXPERTS].astype(I32)
    padded = (cnt + rows - 1) // rows * rows
    pad_end = jnp.cumsum(padded).astype(I32)
    pad_start = pad_end - padded
    hot = e[:, :, None] == jnp.arange(N_EXPERTS, dtype=I32)
    dest = jnp.sum(jnp.where(hot, pad_start, 0), axis=-1).astype(I32) + rank
    nb_used = (pad_end[-1:] // rows).astype(I32)
    blk_exp = jnp.clip(jnp.searchsorted(pad_end, jnp.arange(nb, dtype=I32) * rows, side="right"),
                       0, N_EXPERTS - 1).astype(I32)
    last_blk_row = jnp.where(cnt > 0, pad_end - rows, -1).astype(I32)
    return dest, blk_exp, nb_used, last_blk_row


def _dispatch_body(last_ref, nbu_ref, dest_ref, hp_hbm, xs_hbm, zbuf, sem_z, sem, *, tm, rows, nb):
    i = pl.program_id(0)

    @pl.when(i == 0)
    def _():
        zbuf[...] = jnp.zeros(zbuf.shape, U32)

        def expert_fill(e):
            return pltpu.make_async_copy(zbuf, xs_hbm.at[pl.ds(pl.multiple_of(last_ref[e], rows), rows)], sem_z)

        def tail_fill(b):
            return pltpu.make_async_copy(zbuf, xs_hbm.at[pl.ds(pl.multiple_of(b * rows, rows), rows)], sem_z)

        for wait in (False, True):
            def experts(e, c, wait=wait):
                @pl.when(last_ref[e] >= 0)
                def _():
                    expert_fill(e).wait() if wait else expert_fill(e).start()
                return c

            def tail(b, c, wait=wait):
                @pl.when(b >= nbu_ref[0])
                def _():
                    tail_fill(b).wait() if wait else tail_fill(b).start()
                return c

            lax.fori_loop(0, N_EXPERTS, experts, 0)
            lax.fori_loop(0, nb, tail, 0)

    def row_copy(r, k):
        return pltpu.make_async_copy(hp_hbm.at[pl.ds(i * tm + r, 1)], xs_hbm.at[pl.ds(dest_ref[0, TOP_K * r + k], 1)],
                                     sem)

    for wait in (False, True):
        def rows_loop(r, c, wait=wait):
            for k in range(TOP_K):
                row_copy(r, k).wait() if wait else row_copy(r, k).start()
            return c

        lax.fori_loop(0, tm, rows_loop, 0)


def _dispatch(hp, dest, last_blk_row, nb_used, nb, tm=256):
    t, w = hp.shape
    tm = min(tm, t)
    rows = MOE_ROWS
    grid_spec = pltpu.PrefetchScalarGridSpec(
        num_scalar_prefetch=2,
        grid=(t // tm,),
        in_specs=[pl.BlockSpec((None, 1, TOP_K * tm), lambda i, lr, nu: (i, 0, 0), memory_space=pltpu.SMEM),
                  pl.BlockSpec(memory_space=pl.ANY)],
        out_specs=pl.BlockSpec(memory_space=pl.ANY),
        scratch_shapes=[pltpu.VMEM((rows, w), U32), pltpu.SemaphoreType.DMA(()), pltpu.SemaphoreType.DMA(())],
    )
    return pl.pallas_call(
        functools.partial(_dispatch_body, tm=tm, rows=rows, nb=nb),
        grid_spec=grid_spec,
        out_shape=jax.ShapeDtypeStruct((nb * rows, w), U32),
        compiler_params=_cparams(("arbitrary",), 16),
        name="moe_dispatch",
    )(last_blk_row, nb_used, dest.reshape(t // tm, 1, TOP_K * tm), hp)


def _expert_body(blk_exp_ref, nbu_ref, xs_ref, w1_ref, w3_ref, w2_ref, ys_ref):
    del blk_exp_ref
    b = pl.program_id(0)

    @pl.when(b < nbu_ref[0])
    def _():
        xn = _unpack_pairs(xs_ref[...]).astype(BF16)
        a1 = jnp.dot(xn, w1_ref[...], preferred_element_type=F32)
        a3 = jnp.dot(xn, w3_ref[...], preferred_element_type=F32)
        hmid = (jax.nn.silu(a1) * a3).astype(BF16)
        ys_ref[...] = _pack_pairs(jnp.dot(hmid, w2_ref[...], preferred_element_type=F32))

    @pl.when(b >= nbu_ref[0])
    def _():
        ys_ref[...] = jnp.zeros(ys_ref.shape, U32)


def _experts(xs, blk_exp, nb_used, w1, w3, w2):
    p, w = xs.shape
    _, d, de = w1.shape
    rows = MOE_ROWS
    grid_spec = pltpu.PrefetchScalarGridSpec(
        num_scalar_prefetch=2,
        grid=(p // rows,),
        in_specs=[
            pl.BlockSpec((rows, w), lambda b, be, nu: (jnp.minimum(b, nu[0] - 1), 0)),
            pl.BlockSpec((None, d, de), lambda b, be, nu: (be[b], 0, 0)),
            pl.BlockSpec((None, d, de), lambda b, be, nu: (be[b], 0, 0)),
            pl.BlockSpec((None, de, d), lambda b, be, nu: (be[b], 0, 0)),
        ],
        out_specs=pl.BlockSpec((rows, w), lambda b, be, nu: (b, 0)),
    )
    return pl.pallas_call(
        _expert_body,
        grid_spec=grid_spec,
        out_shape=jax.ShapeDtypeStruct((p, w), U32),
        compiler_params=_cparams(("arbitrary",), 40),
        name="moe_experts",
    )(blk_exp, nb_used, xs, w1, w3, w2)


def _combine_body(dest_ref, x_ref, gates_ref, g_ref, ys_hbm, o_ref, ybuf, sem, *, tm, final):
    def row_copy(r, k):
        return pltpu.make_async_copy(ys_hbm.at[pl.ds(dest_ref[0, TOP_K * r + k], 1)], ybuf.at[k, pl.ds(r, 1)], sem)

    for wait in (False, True):
        def rows_loop(r, c, wait=wait):
            for k in range(TOP_K):
                row_copy(r, k).wait() if wait else row_copy(r, k).start()
            return c

        lax.fori_loop(0, tm, rows_loop, 0)

    gates = gates_ref[...]
    x = x_ref[...] + (gates[:, 0:1] * _unpack_pairs(ybuf[0]) + gates[:, 1:2] * _unpack_pairs(ybuf[1]))
    o_ref[...] = _rms(x, g_ref[...]) if final else x


def _combine(x, ys, dest, gates, gain, final, tm=256):
    t, d = x.shape
    tm = min(tm, t)
    w = ys.shape[1]
    return pl.pallas_call(
        functools.partial(_combine_body, tm=tm, final=final),
        grid=(t // tm,),
        in_specs=[pl.BlockSpec((None, 1, TOP_K * tm), lambda i: (i, 0, 0), memory_space=pltpu.SMEM),
                  pl.BlockSpec((tm, d), lambda i: (i, 0)),
                  pl.BlockSpec((tm, LANES), lambda i: (i, 0)),
                  pl.BlockSpec((1, d), lambda i: (0, 0)),
                  pl.BlockSpec(memory_space=pl.ANY)],
        out_specs=pl.BlockSpec((tm, d), lambda i: (i, 0)),
        out_shape=jax.ShapeDtypeStruct((t, d), F32),
        scratch_shapes=[pltpu.VMEM((TOP_K, tm, w), U32), pltpu.SemaphoreType.DMA(())],
        compiler_params=_cparams(("arbitrary",), 40),
        name="moe_combine",
    )(dest.reshape(t // tm, 1, TOP_K * tm), x, gates, gain.reshape(1, d), ys)


def _rope_cols(w_rope):
    half = QK_ROPE // 2
    x1, x2 = w_rope[..., :half], w_rope[..., half:]
    return jnp.concatenate([x1, x2, x2, x1], axis=-1)


def _layer_weights(w_in, w_uq, w_ukv, w_router_g, b_router_g, w_router_e, b_router_e):
    d = w_in.shape[0]
    o = np.cumsum([0, Q_LORA, KV_LORA, QK_ROPE, HB * 2 * DH_B, HB * 2 * DH_B, HB * 2 * DH_B, d, d])
    w_cq, w_ckv, w_kr, w_qb, w_kb, w_vb, w_ga, w_gb = (w_in[:, o[n]:o[n + 1]] for n in range(8))
    pad = jnp.zeros((d, 2048 - Q_LORA - KV_LORA - LANES), w_in.dtype)
    w_lat = jnp.concatenate([w_cq, w_ckv, _rope_cols(w_kr), pad], axis=1).astype(BF16)
    w_qkb = jnp.concatenate([w_qb, w_kb], axis=1).astype(BF16)
    w_gate = jnp.concatenate([w_ga, w_gb], axis=1).astype(BF16)
    wq = w_uq.reshape(Q_LORA, HA, QK_NOPE + QK_ROPE)
    wq = jnp.concatenate([wq[..., :QK_NOPE], _rope_cols(wq[..., QK_NOPE:])], axis=-1).reshape(Q_LORA, HA * 2 * LANES)
    w_r = jnp.concatenate([w_router_e, w_router_g, jnp.zeros((d, LANES - N_EXPERTS - N_GROUPS), F32)], axis=1)
    w_r_hi = w_r.astype(BF16)
    w_r_lo = (w_r - w_r_hi.astype(F32)).astype(BF16)
    b_r = jnp.concatenate([b_router_e, b_router_g, jnp.zeros((LANES - N_EXPERTS - N_GROUPS,), F32)]).reshape(1, LANES)
    return w_lat, w_qkb, w_vb.astype(BF16), w_gate, wq.astype(BF16), w_ukv.astype(BF16), w_r_hi, w_r_lo, b_r


def kernel(x, positions, norm_attn, w_in, q_norm, w_uq, kv_norm, w_ukv, lam_q1, lam_k1, lam_q2, lam_k2, subln, w_oa,
           w_ob, w_out, norm_ffn, w_router_g, b_router_g, w_router_e, b_router_e, w1, w3, w2, norm_final):
    bsz, s, d = x.shape
    assert bsz == 1
    depth = w_in.shape[0]
    x = x.reshape(s, d)
    pos_col = positions.reshape(s, 1)
    table = _rope_table(pos_col)
    pos_col_f = pos_col.astype(F32)
    pos_row_f = pos_col_f.reshape(1, s)
    slopes_l2 = jnp.asarray(2.0 ** (-8.0 * (np.arange(HB) + 1) / HB) * LOG2E, dtype=F32)
    c_a = (QK_NOPE + QK_ROPE) ** -0.5 * LOG2E
    c_b = DH_B ** -0.5 * LOG2E
    tm = min(1024, s)
    att_t = min(ATT_T, s)
    hd = 2 * LANES

    def chunked_transpose(acc):
        t = jnp.concatenate([acc.T, jnp.ones((ONES_ROWS, tm), F32)], axis=0)
        return (jnp.stack([t[:, c * att_t:(c + 1) * att_t] for c in range(tm // att_t)]),)

    for l in range(depth):
        w_lat, w_qkb, w_vb, w_gate, wq, wkv, w_r_hi, w_r_lo, b_r = _layer_weights(
            w_in[l], w_uq[l], w_ukv[l], w_router_g[l], b_router_g[l], w_router_e[l], b_router_e[l])
        h = _rmsnorm(x, norm_attn[l], BF16)
        full = ((tm, d), lambda i, j: (i, 0))

        def ident(acc):
            return (acc,)

        (lat,) = _matmul("in_proj_latent", h, *full, w_lat, 512,
                         [(jax.ShapeDtypeStruct((s, 2048), BF16), (tm, 512), lambda i, j: (i, j))], ident)

        def scale_q(acc):
            return (acc * jnp.where(pl.program_id(1) < HB, c_b, 1.0),)

        (qk_b,) = _matmul("in_proj_qk_b", h, *full, w_qkb, hd,
                          [(jax.ShapeDtypeStruct((2 * HB, s, hd), BF16), (None, tm, hd), lambda i, j: (j, i, 0))],
                          scale_q)
        (vt_b,) = _matmul("in_proj_v_b", h, *full, w_vb, hd,
                          [(jax.ShapeDtypeStruct((HB, s // att_t, hd + ONES_ROWS, att_t), BF16),
                            (None, tm // att_t, hd + ONES_ROWS, att_t),
                            lambda i, j: (j, i, 0, 0))], chunked_transpose)
        (gates,) = _matmul("in_proj_gates", h, *full, w_gate, 512,
                           [(jax.ShapeDtypeStruct((s, 2 * d), BF16), (tm, 512), lambda i, j: (i, j))], ident)

        def q_epilogue(acc, t_ref):
            rot = _rope_128(acc[:, LANES:], t_ref[...])
            return (jnp.concatenate([acc[:, :LANES], rot], axis=1) * c_a,)

        (q_a,) = _matmul("q_up_proj", lat, (tm, Q_LORA), lambda i, j: (i, 0), wq, hd,
                         [(jax.ShapeDtypeStruct((HA, s, hd), BF16), (None, tm, hd), lambda i, j: (j, i, 0))],
                         q_epilogue, gain=q_norm[l], extras=[(table, (tm, LANES), lambda i, j: (i, 0))])

        def kv_epilogue(acc, kr_ref, t_ref):
            rot = _rope_128(kr_ref[...].astype(F32), t_ref[...])
            lane = lax.broadcasted_iota(I32, rot.shape, 1)
            rot = jnp.where(lane < QK_ROPE, rot, 0.0)
            return jnp.concatenate([acc[:, :LANES], rot], axis=1), chunked_transpose(acc[:, LANES:])[0]

        k_a, vt_a = _matmul("kv_up_proj", lat, (tm, KV_LORA), lambda i, j: (i, Q_LORA // KV_LORA), wkv, hd,
                            [(jax.ShapeDtypeStruct((HA, s, hd), BF16), (None, tm, hd), lambda i, j: (j, i, 0)),
                             (jax.ShapeDtypeStruct((HA, s // att_t, V_DIM_A + ONES_ROWS, att_t), BF16),
                              (None, tm // att_t, V_DIM_A + ONES_ROWS, att_t), lambda i, j: (j, i, 0, 0))],
                            kv_epilogue, gain=kv_norm[l],
                            extras=[(lat, (tm, LANES), lambda i, j: (i, (Q_LORA + KV_LORA) // LANES)),
                                    (table, (tm, LANES), lambda i, j: (i, 0))])
        o_a = _mla_attention(q_a, k_a, vt_a)

        lam_init = 0.8 - 0.6 * math.exp(-0.3 * l)
        lam_vecs = [v[l].reshape(1, DH_B).astype(F32) for v in (lam_q1, lam_k1, lam_q2, lam_k2)]
        o_b = _diff_attention(qk_b, vt_b, pos_col_f, pos_row_f, slopes_l2, lam_vecs, subln[l], lam_init)

        merged = _merge(o_a, o_b, w_oa[l].astype(BF16), w_ob[l].astype(BF16), gates)

        def residual(acc, x_ref):
            return (x_ref[...] + acc,)

        (x,) = _matmul("out_proj", merged, (tm, d), lambda i, j: (i, 0), w_out[l].astype(BF16), 512,
                       [(jax.ShapeDtypeStruct((s, d), F32), (tm, 512), lambda i, j: (i, j))], residual,
                       extras=[(x, (tm, 512), lambda i, j: (i, j))])

        ids, gate_vals, hp, counts = _router(x, norm_ffn[l], w_r_hi, w_r_lo, b_r)
        dest, blk_exp, nb_used, last_blk_row = _dispatch_plan(ids, counts, s)
        xs = _dispatch(hp, dest, last_blk_row, nb_used, blk_exp.shape[0])
        ys = _experts(xs, blk_exp, nb_used, w1[l].astype(BF16), w3[l].astype(BF16), w2[l].astype(BF16))
        final = l == depth - 1
        x = _combine(x, ys, dest, gate_vals, norm_final if final else norm_ffn[l], final)
    return x.reshape(bsz, s, d)
```

```python
import functools
import math

import jax
import jax.numpy as jnp
import numpy as np
from jax import lax
from jax.experimental import pallas as pl
from jax.experimental.pallas import tpu as pltpu

F32 = jnp.float32
BF16 = jnp.bfloat16
I32 = jnp.int32
U32 = jnp.uint32

EPS = 1e-6
ROPE_THETA = 10000.0
HA, QK_NOPE, QK_ROPE, V_DIM_A = 16, 128, 64, 128
Q_LORA, KV_LORA = 1024, 512
HB, DH_B = 8, 128
N_GROUPS, EXP_PER_GROUP, TOP_K = 8, 8, 2
N_EXPERTS = N_GROUPS * EXP_PER_GROUP
LOG2E = 1.4426950408889634
NEG = -1e30

LANES = 128
MOE_ROWS = 128
ATT_T = 512
ATT_TQ = 1024
ATT_UNROLL = 2
ONES_ROWS = 16
MIB = 1024 * 1024


def _cparams(semantics, vmem_mib):
    return pltpu.CompilerParams(dimension_semantics=semantics, vmem_limit_bytes=vmem_mib * MIB)


def _rms(xf, gain):
    ms = jnp.mean(xf * xf, axis=-1, keepdims=True)
    return xf * lax.rsqrt(ms + EPS) * gain


def _rmsnorm_body(x_ref, g_ref, o_ref):
    o_ref[...] = _rms(x_ref[...], g_ref[...]).astype(o_ref.dtype)


def _rmsnorm(x, gain, out_dtype, tm=256):
    s, d = x.shape
    tm = min(tm, s)
    return pl.pallas_call(
        _rmsnorm_body,
        grid=(s // tm,),
        in_specs=[pl.BlockSpec((tm, d), lambda i: (i, 0)), pl.BlockSpec((1, d), lambda i: (0, 0))],
        out_specs=pl.BlockSpec((tm, d), lambda i: (i, 0)),
        out_shape=jax.ShapeDtypeStruct((s, d), out_dtype),
        compiler_params=_cparams(("parallel",), 32),
        name="rmsnorm",
    )(x, gain.reshape(1, d))


def _matmul_body(*refs, n_extra, n_out, norm, epilogue):
    a_ref = refs[0]
    pos = 1
    if norm:
        g_ref = refs[pos]
        pos += 1
    b_ref = refs[pos]
    pos += 1
    extras = refs[pos:pos + n_extra]
    pos += n_extra
    outs = refs[pos:pos + n_out]
    pos += n_out
    if norm:
        lhs_ref = refs[pos]

        @pl.when(pl.program_id(1) == 0)
        def _():
            lhs_ref[...] = _rms(a_ref[...].astype(F32), g_ref[...]).astype(BF16)

        lhs = lhs_ref[...]
    else:
        lhs = a_ref[...]
    acc = jnp.dot(lhs, b_ref[...], preferred_element_type=F32)
    for o_ref, r in zip(outs, epilogue(acc, *extras)):
        o_ref[...] = r.astype(o_ref.dtype)


def _matmul(name, a, a_block, a_index, b, tn, outs, epilogue, gain=None, extras=(), vmem_mib=48):
    tm, k = a_block
    m = a.shape[0]
    n = b.shape[1]
    norm = gain is not None
    in_specs = [pl.BlockSpec(a_block, a_index)]
    args = [a]
    if norm:
        in_specs.append(pl.BlockSpec((1, k), lambda i, j: (0, 0)))
        args.append(gain.reshape(1, k).astype(F32))
    in_specs.append(pl.BlockSpec((k, tn), lambda i, j: (0, j)))
    args.append(b)
    for arr, blk, imap in extras:
        in_specs.append(pl.BlockSpec(blk, imap))
        args.append(arr)
    res = pl.pallas_call(
        functools.partial(_matmul_body, n_extra=len(extras), n_out=len(outs), norm=norm, epilogue=epilogue),
        grid=(m // tm, n // tn),
        in_specs=in_specs,
        out_specs=[pl.BlockSpec(blk, imap) for _, blk, imap in outs],
        out_shape=[sds for sds, _, _ in outs],
        scratch_shapes=[pltpu.VMEM((tm, k), BF16)] if norm else [],
        compiler_params=_cparams(("parallel", "arbitrary"), vmem_mib),
        name=name,
    )(*args)
    return res


def _rope_table_body(pos_ref, inv_ref, t_ref):
    ang = pos_ref[...].astype(F32) * inv_ref[...]
    c = jnp.cos(ang)
    s = jnp.sin(ang)
    lane = lax.broadcasted_iota(I32, ang.shape, 1)
    t_ref[...] = jnp.where(lane < 2 * (QK_ROPE // 2), c, jnp.where(lane < 3 * (QK_ROPE // 2), -s, s))


def _rope_table(positions_col, tm=512):
    s = positions_col.shape[0]
    tm = min(tm, s)
    half = QK_ROPE // 2
    inv = ROPE_THETA ** (-jnp.arange(half, dtype=F32) / half)
    inv4 = jnp.tile(inv, 4).reshape(1, LANES)
    return pl.pallas_call(
        _rope_table_body,
        grid=(s // tm,),
        in_specs=[pl.BlockSpec((tm, 1), lambda i: (i, 0)), pl.BlockSpec((1, LANES), lambda i: (0, 0))],
        out_specs=pl.BlockSpec((tm, LANES), lambda i: (i, 0)),
        out_shape=jax.ShapeDtypeStruct((s, LANES), F32),
        compiler_params=_cparams(("parallel",), 16),
        name="rope_table",
    )(positions_col, inv4)


def _rope_128(v, table):
    t = v * table
    return t + pltpu.roll(t, 2 * (QK_ROPE // 2), axis=1)


def _softmax_step_t(st, vt, m_ref, acc_ref):
    m_prev = m_ref[...]
    m_new = jnp.maximum(m_prev, jnp.max(st, axis=0, keepdims=True))
    alpha = jnp.exp2(m_prev - m_new)
    p = jnp.exp2(st - m_new)
    acc_ref[...] = acc_ref[...] * alpha + jnp.dot(vt, p.astype(BF16), preferred_element_type=F32)
    m_ref[...] = m_new


def _kq(k, q):
    return lax.dot_general(k, q, (((1,), (1,)), ((), ())), preferred_element_type=F32)


def _causal_mask_t(st, key0, query0):
    key = key0 + lax.broadcasted_iota(I32, st.shape, 0)
    qry = query0 + lax.broadcasted_iota(I32, st.shape, 1)
    return jnp.where(key <= qry, st, NEG)


def _init_stats(m_ref, acc_ref):
    m_ref[...] = jnp.full(m_ref.shape, NEG, F32)
    acc_ref[...] = jnp.zeros(acc_ref.shape, F32)


def _normalized(acc_ref, chain, dv):
    return acc_ref[chain, :dv, :] / acc_ref[chain, dv:dv + 1, :]


def _causal_pipeline(n_common, n_diag, chains, scores, update, s_ref, unroll=ATT_UNROLL):
    assert unroll % 2 == 0

    def ahead(c, slot):
        for a, st in zip(chains, scores(c)):
            s_ref[2 * a + slot] = st

    def fold(c, slot, masked=False):
        for a in chains:
            update(c, a, s_ref[2 * a + slot], masked)

    def step(c, t):
        ahead(c + 1, (t + 1) % 2)
        fold(c, t % 2)

    ahead(0, 0)
    n_groups = n_common // unroll

    def body(g, carry):
        for t in range(unroll):
            step(g * unroll + t, t)
        return carry

    lax.fori_loop(0, n_groups, body, 0)
    c0 = n_groups * unroll
    rem = n_common - c0
    for t in range(unroll - 1):
        @pl.when(t < rem)
        def _(t=t):
            step(c0 + t, t)

    for parity in range(2):
        @pl.when(rem % 2 == parity)
        def _(parity=parity):
            for d in range(n_diag):
                if d + 1 < n_diag:
                    ahead(n_common + d + 1, (parity + d + 1) % 2)
                fold(n_common + d, (parity + d) % 2, True)


def _mla_body(q_ref, k_ref, vt_ref, o_ref, m_ref, acc_ref, s_ref, *, tq, tk):
    i = pl.program_id(1)
    _init_stats(m_ref, acc_ref)
    q = q_ref[...]

    def scores(c):
        return [_kq(k_ref[pl.ds(pl.multiple_of(c * tk, tk), tk), :], q)]

    def update(c, a, st, masked):
        if masked:
            st = _causal_mask_t(st, c * tk, i * tq)
        _softmax_step_t(st, vt_ref[c], m_ref.at[a], acc_ref.at[a])

    _causal_pipeline(i * (tq // tk), tq // tk, [0], scores, update, s_ref)
    o_ref[...] = _normalized(acc_ref, 0, o_ref.shape[1]).T.astype(o_ref.dtype)


def _mla_attention(q, k, vt):
    h, s, dk = q.shape
    _, nchunk, dvp, tk = vt.shape
    dv = dvp - ONES_ROWS
    tq = max(min(ATT_TQ, s), tk)
    return pl.pallas_call(
        functools.partial(_mla_body, tq=tq, tk=tk),
        grid=(h, s // tq),
        in_specs=[
            pl.BlockSpec((None, tq, dk), lambda hh, i: (hh, i, 0)),
            pl.BlockSpec((None, s, dk), lambda hh, i: (hh, 0, 0)),
            pl.BlockSpec((None, nchunk, dvp, tk), lambda hh, i: (hh, 0, 0, 0)),
        ],
        out_specs=pl.BlockSpec((tq, dv), lambda hh, i: (i, hh)),
        out_shape=jax.ShapeDtypeStruct((s, h * dv), BF16),
        scratch_shapes=[pltpu.VMEM((1, 1, tq), F32), pltpu.VMEM((1, dvp, tq), F32), pltpu.VMEM((2, tk, tq), F32)],
        compiler_params=_cparams(("parallel", "arbitrary"), 48),
        name="mla_attention",
    )(q, k, vt)


def _diff_body(slope_ref, q_ref, k_ref, vt_ref, pcol_ref, prow_ref, lq1_ref, lk1_ref, lq2_ref, lk2_ref, sub_ref,
               o_ref, m_ref, acc_ref, s_ref, *, tq, tk, lam_init):
    h = pl.program_id(0)
    i = pl.program_id(1)
    _init_stats(m_ref, acc_ref)
    qs = [q_ref[:, c * DH_B:(c + 1) * DH_B] for c in range(2)]
    pq = prow_ref[...]
    slope = slope_ref[h]

    def scores(c):
        start = pl.multiple_of(c * tk, tk)
        k = k_ref[pl.ds(start, tk), :]
        bias = jnp.abs(pcol_ref[pl.ds(start, tk), :] - pq) * slope
        return [_kq(k[:, mp * DH_B:(mp + 1) * DH_B], qs[mp]) - bias for mp in range(2)]

    def update(c, chain, st, masked):
        if masked:
            st = _causal_mask_t(st, c * tk, i * tq)
        _softmax_step_t(st, vt_ref[c], m_ref.at[chain], acc_ref.at[chain])

    _causal_pipeline(i * (tq // tk), tq // tk, [0, 1], scores, update, s_ref)

    lam = (jnp.exp(jnp.sum(lq1_ref[...] * lk1_ref[...], axis=1, keepdims=True))
           - jnp.exp(jnp.sum(lq2_ref[...] * lk2_ref[...], axis=1, keepdims=True)) + lam_init)
    dv = o_ref.shape[1]
    o = _normalized(acc_ref, 0, dv) - lam * _normalized(acc_ref, 1, dv)
    ms = jnp.mean(o * o, axis=0, keepdims=True)
    y = o * lax.rsqrt(ms + EPS) * sub_ref[...] * (1.0 - lam_init)
    o_ref[...] = y.T.astype(o_ref.dtype)


def _diff_attention(qk, vt, pos_col, pos_row, slopes_l2, lam_vecs, subln, lam_init):
    _, s, dqk = qk.shape
    _, nchunk, dvp, tk = vt.shape
    dv = dvp - ONES_ROWS
    tq = max(min(ATT_TQ, s), tk)
    vec = pl.BlockSpec((1, DH_B), lambda hh, i: (0, 0))
    once = pl.Buffered(1)
    return pl.pallas_call(
        functools.partial(_diff_body, tq=tq, tk=tk, lam_init=lam_init),
        grid=(HB, s // tq),
        in_specs=[
            pl.BlockSpec(memory_space=pltpu.SMEM),
            pl.BlockSpec((None, tq, dqk), lambda hh, i: (hh, i, 0)),
            pl.BlockSpec((None, s, dqk), lambda hh, i: (HB + hh, 0, 0), pipeline_mode=once),
            pl.BlockSpec((None, nchunk, dvp, tk), lambda hh, i: (hh, 0, 0, 0), pipeline_mode=once),
            pl.BlockSpec((s, 1), lambda hh, i: (0, 0), pipeline_mode=once),
            pl.BlockSpec((1, tq), lambda hh, i: (0, i)),
            vec, vec, vec, vec,
            pl.BlockSpec((dv, 1), lambda hh, i: (0, 0)),
        ],
        out_specs=pl.BlockSpec((tq, dv), lambda hh, i: (i, hh)),
        out_shape=jax.ShapeDtypeStruct((s, HB * dv), BF16),
        scratch_shapes=[pltpu.VMEM((2, 1, tq), F32), pltpu.VMEM((2, dvp, tq), F32), pltpu.VMEM((4, tk, tq), F32)],
        compiler_params=_cparams(("parallel", "arbitrary"), 56),
        name="diff_attention",
    )(slopes_l2, qk, qk, vt, pos_col, pos_row, *lam_vecs, subln.reshape(dv, 1))


def _merge_body(oa_ref, ob_ref, woa_ref, wob_ref, ga_ref, gb_ref, o_ref):
    ya = jnp.dot(oa_ref[...], woa_ref[...], preferred_element_type=F32)
    yb = jnp.dot(ob_ref[...], wob_ref[...], preferred_element_type=F32)
    out = jax.nn.sigmoid(ga_ref[...].astype(F32)) * ya + jax.nn.sigmoid(gb_ref[...].astype(F32)) * yb
    o_ref[...] = out.astype(o_ref.dtype)


def _merge(o_a, o_b, w_oa, w_ob, gates, tm=1024, tn=512):
    s, ka = o_a.shape
    kb = o_b.shape[1]
    d = w_oa.shape[1]
    tm = min(tm, s)
    nj = d // tn
    return pl.pallas_call(
        _merge_body,
        grid=(s // tm, nj),
        in_specs=[
            pl.BlockSpec((tm, ka), lambda i, j: (i, 0)),
            pl.BlockSpec((tm, kb), lambda i, j: (i, 0)),
            pl.BlockSpec((ka, tn), lambda i, j: (0, j)),
            pl.BlockSpec((kb, tn), lambda i, j: (0, j)),
            pl.BlockSpec((tm, tn), lambda i, j: (i, j)),
            pl.BlockSpec((tm, tn), lambda i, j: (i, nj + j)),
        ],
        out_specs=pl.BlockSpec((tm, tn), lambda i, j: (i, j)),
        out_shape=jax.ShapeDtypeStruct((s, d), BF16),
        compiler_params=_cparams(("parallel", "arbitrary"), 48),
        name="gated_merge",
    )(o_a, o_b, w_oa, w_ob, gates, gates)


def _pack_pairs(x):
    half = x.shape[1] // 2
    lo = pltpu.bitcast(x[:, :half].astype(BF16).astype(F32), U32)
    hi = pltpu.bitcast(x[:, half:].astype(BF16).astype(F32), U32)
    return (lo >> 16) | (hi & jnp.uint32(0xFFFF0000))


def _unpack_pairs(p):
    lo = pltpu.bitcast(p << 16, F32)
    hi = pltpu.bitcast(p & jnp.uint32(0xFFFF0000), F32)
    return jnp.concatenate([lo, hi], axis=1)


def _router_body(x_ref, g_ref, whi_ref, wlo_ref, b_ref, tri_ref, ids_ref, gates_ref, hp_ref, cnt_ref):
    @pl.when(pl.program_id(0) == 0)
    def _():
        cnt_ref[...] = jnp.zeros(cnt_ref.shape, F32)

    h = _rms(x_ref[...], g_ref[...])
    hp_ref[...] = _pack_pairs(h)
    h_hi = h.astype(BF16)
    h_lo = (h - h_hi.astype(F32)).astype(BF16)
    logits = (jnp.dot(h_hi, whi_ref[...], preferred_element_type=F32)
              + jnp.dot(h_lo, whi_ref[...], preferred_element_type=F32)
              + jnp.dot(h_hi, wlo_ref[...], preferred_element_type=F32)) + b_ref[...]
    lane = lax.broadcasted_iota(I32, logits.shape, 1).astype(F32)

    def first_argmax(vals, mx):
        return jnp.min(jnp.where(vals == mx, lane, 4.0 * LANES), axis=1, keepdims=True)

    is_g = jnp.logical_and(lane >= N_EXPERTS, lane < N_EXPERTS + N_GROUPS)
    lg = jnp.where(is_g, logits, NEG)
    mg = jnp.max(lg, axis=1, keepdims=True)
    p_g = 1.0 / jnp.sum(jnp.exp(lg - mg), axis=1, keepdims=True)
    g_idx = first_argmax(lg, mg) - N_EXPERTS
    lo = g_idx * EXP_PER_GROUP
    in_grp = jnp.logical_and(lane >= lo, lane < lo + EXP_PER_GROUP)
    le = jnp.where(in_grp, logits, NEG)
    m1 = jnp.max(le, axis=1, keepdims=True)
    i1 = first_argmax(le, m1)
    den = jnp.sum(jnp.exp(le - m1), axis=1, keepdims=True)
    le2 = jnp.where(lane == i1, NEG, le)
    m2 = jnp.max(le2, axis=1, keepdims=True)
    i2 = first_argmax(le2, m2)
    p1 = 1.0 / den
    p2 = jnp.exp(m2 - m1) / den
    psum = p1 + p2
    gate1 = p_g * p1 / psum
    gate2 = p_g * p2 / psum

    hot1 = jnp.where(lane == i1, 1.0, 0.0)
    hot2 = jnp.where(lane == i2, 1.0, 0.0)
    tri = tri_ref[...]
    before1 = jnp.dot(tri, hot1.astype(BF16), preferred_element_type=F32) + cnt_ref[...]
    tot1 = jnp.sum(hot1, axis=0, keepdims=True)
    before2 = jnp.dot(tri, hot2.astype(BF16), preferred_element_type=F32) + (cnt_ref[...] + tot1)
    r1 = jnp.sum(hot1 * before1, axis=1, keepdims=True)
    r2 = jnp.sum(hot2 * before2, axis=1, keepdims=True)
    cnt_ref[...] = cnt_ref[...] + tot1 + jnp.sum(hot2, axis=0, keepdims=True)

    ids = jnp.where(lane == 0, i1, jnp.where(lane == 1, i2, jnp.where(lane == 2, r1, jnp.where(lane == 3, r2, 0.0))))
    ids_ref[...] = ids.astype(I32)
    gates_ref[...] = jnp.where(lane == 0, gate1, jnp.where(lane == 1, gate2, 0.0))


def _router(x, gain, w_hi, w_lo, bias, tm=256):
    t, d = x.shape
    tm = min(tm, t)
    tri = jnp.tril(jnp.ones((tm, tm), F32), -1).astype(BF16)
    return pl.pallas_call(
        _router_body,
        grid=(t // tm,),
        in_specs=[
            pl.BlockSpec((tm, d), lambda i: (i, 0)),
            pl.BlockSpec((1, d), lambda i: (0, 0)),
            pl.BlockSpec((d, LANES), lambda i: (0, 0)),
            pl.BlockSpec((d, LANES), lambda i: (0, 0)),
            pl.BlockSpec((1, LANES), lambda i: (0, 0)),
            pl.BlockSpec((tm, tm), lambda i: (0, 0)),
        ],
        out_specs=[pl.BlockSpec((tm, LANES), lambda i: (i, 0)), pl.BlockSpec((tm, LANES), lambda i: (i, 0)),
                   pl.BlockSpec((tm, d // 2), lambda i: (i, 0)), pl.BlockSpec((1, LANES), lambda i: (0, 0))],
        out_shape=[jax.ShapeDtypeStruct((t, LANES), I32), jax.ShapeDtypeStruct((t, LANES), F32),
                   jax.ShapeDtypeStruct((t, d // 2), U32), jax.ShapeDtypeStruct((1, LANES), F32)],
        compiler_params=_cparams(("arbitrary",), 40),
        name="moe_router",
    )(x, gain.reshape(1, d), w_hi, w_lo, bias, tri)


def _dispatch_plan(ids, counts, t):
    rows = MOE_ROWS
    nb = t * TOP_K // rows + N_EXPERTS
    e = ids[:, :TOP_K]
    rank = ids[:, TOP_K:2 * TOP_K]
    cnt = counts[0, :N_EXPERTS].astype(I32)
    padded = (cnt + rows - 1) // rows * rows
    experts = jnp.arange(N_EXPERTS, dtype=I32)
    pad_end = jnp.sum(jnp.where(experts[None, :] <= experts[:, None], padded[None, :], 0), axis=1).astype(I32)
    pad_start = pad_end - padded
    hot = e[:, :, None] == experts
    dest = jnp.sum(jnp.where(hot, pad_start, 0), axis=-1).astype(I32) + rank
    nb_used = (pad_end[-1:] // rows).astype(I32)
    blk_row = jnp.arange(nb, dtype=I32) * rows
    blk_exp = jnp.minimum(jnp.sum(pad_end[None, :] <= blk_row[:, None], axis=1), N_EXPERTS - 1).astype(I32)
    last_blk_row = jnp.where(cnt > 0, pad_end - rows, -1).astype(I32)
    return dest, blk_exp, nb_used, last_blk_row


def _dispatch_body(last_ref, nbu_ref, dest_ref, hp_ref, xs_hbm, zbuf, sem_z, sem, *, tm, rows, nb):
    i = pl.program_id(0)

    @pl.when(i == 0)
    def _():
        zbuf[...] = jnp.zeros(zbuf.shape, U32)

        def expert_fill(e):
            return pltpu.make_async_copy(zbuf, xs_hbm.at[pl.ds(pl.multiple_of(last_ref[e], rows), rows)], sem_z)

        def tail_fill(b):
            return pltpu.make_async_copy(zbuf, xs_hbm.at[pl.ds(pl.multiple_of(b * rows, rows), rows)], sem_z)

        for wait in (False, True):
            def experts(e, c, wait=wait):
                @pl.when(last_ref[e] >= 0)
                def _():
                    expert_fill(e).wait() if wait else expert_fill(e).start()
                return c

            def tail(b, c, wait=wait):
                @pl.when(b >= nbu_ref[0])
                def _():
                    tail_fill(b).wait() if wait else tail_fill(b).start()
                return c

            lax.fori_loop(0, N_EXPERTS, experts, 0)
            lax.fori_loop(0, nb, tail, 0)

    def row_copy(r, k):
        return pltpu.make_async_copy(hp_ref.at[pl.ds(r, 1)], xs_hbm.at[pl.ds(dest_ref[0, TOP_K * r + k], 1)], sem)

    for wait in (False, True):
        def rows_loop(r, c, wait=wait):
            for k in range(TOP_K):
                row_copy(r, k).wait() if wait else row_copy(r, k).start()
            return c

        lax.fori_loop(0, tm, rows_loop, 0)


def _dispatch(hp, dest, last_blk_row, nb_used, nb, tm=256):
    t, w = hp.shape
    tm = min(tm, t)
    rows = MOE_ROWS
    grid_spec = pltpu.PrefetchScalarGridSpec(
        num_scalar_prefetch=2,
        grid=(t // tm,),
        in_specs=[pl.BlockSpec((None, 1, TOP_K * tm), lambda i, lr, nu: (i, 0, 0), memory_space=pltpu.SMEM),
                  pl.BlockSpec((tm, w), lambda i, lr, nu: (i, 0))],
        out_specs=pl.BlockSpec(memory_space=pl.ANY),
        scratch_shapes=[pltpu.VMEM((rows, w), U32), pltpu.SemaphoreType.DMA(()), pltpu.SemaphoreType.DMA(())],
    )
    return pl.pallas_call(
        functools.partial(_dispatch_body, tm=tm, rows=rows, nb=nb),
        grid_spec=grid_spec,
        out_shape=jax.ShapeDtypeStruct((nb * rows, w), U32),
        compiler_params=_cparams(("arbitrary",), 16),
        name="moe_dispatch",
    )(last_blk_row, nb_used, dest.reshape(t // tm, 1, TOP_K * tm), hp)


def _expert_body(blk_exp_ref, nbu_ref, xs_ref, w1_ref, w3_ref, w2_ref, ys_ref):
    del blk_exp_ref
    b = pl.program_id(0)

    @pl.when(b < nbu_ref[0])
    def _():
        xn = _unpack_pairs(xs_ref[...]).astype(BF16)
        a1 = jnp.dot(xn, w1_ref[...], preferred_element_type=F32)
        a3 = jnp.dot(xn, w3_ref[...], preferred_element_type=F32)
        hmid = (jax.nn.silu(a1) * a3).astype(BF16)
        ys_ref[...] = _pack_pairs(jnp.dot(hmid, w2_ref[...], preferred_element_type=F32))

    @pl.when(b >= nbu_ref[0])
    def _():
        ys_ref[...] = jnp.zeros(ys_ref.shape, U32)


def _experts(xs, blk_exp, nb_used, w1, w3, w2):
    p, w = xs.shape
    _, d, de = w1.shape
    rows = MOE_ROWS
    grid_spec = pltpu.PrefetchScalarGridSpec(
        num_scalar_prefetch=2,
        grid=(p // rows,),
        in_specs=[
            pl.BlockSpec((rows, w), lambda b, be, nu: (jnp.minimum(b, nu[0] - 1), 0)),
            pl.BlockSpec((None, d, de), lambda b, be, nu: (be[b], 0, 0)),
            pl.BlockSpec((None, d, de), lambda b, be, nu: (be[b], 0, 0)),
            pl.BlockSpec((None, de, d), lambda b, be, nu: (be[b], 0, 0)),
        ],
        out_specs=pl.BlockSpec((rows, w), lambda b, be, nu: (b, 0)),
    )
    return pl.pallas_call(
        _expert_body,
        grid_spec=grid_spec,
        out_shape=jax.ShapeDtypeStruct((p, w), U32),
        compiler_params=_cparams(("arbitrary",), 40),
        name="moe_experts",
    )(blk_exp, nb_used, xs, w1, w3, w2)


def _combine_body(dest_ref, x_ref, gates_ref, g_ref, ys_hbm, o_ref, ybuf, sem, *, tm, final):
    def row_copy(r, k):
        return pltpu.make_async_copy(ys_hbm.at[pl.ds(dest_ref[0, TOP_K * r + k], 1)], ybuf.at[k, pl.ds(r, 1)], sem)

    for wait in (False, True):
        def rows_loop(r, c, wait=wait):
            for k in range(TOP_K):
                row_copy(r, k).wait() if wait else row_copy(r, k).start()
            return c

        lax.fori_loop(0, tm, rows_loop, 0)

    gates = gates_ref[...]
    x = x_ref[...] + (gates[:, 0:1] * _unpack_pairs(ybuf[0]) + gates[:, 1:2] * _unpack_pairs(ybuf[1]))
    o_ref[...] = _rms(x, g_ref[...]) if final else x


def _combine(x, ys, dest, gates, gain, final, tm=256):
    t, d = x.shape
    tm = min(tm, t)
    w = ys.shape[1]
    return pl.pallas_call(
        functools.partial(_combine_body, tm=tm, final=final),
        grid=(t // tm,),
        in_specs=[pl.BlockSpec((None, 1, TOP_K * tm), lambda i: (i, 0, 0), memory_space=pltpu.SMEM),
                  pl.BlockSpec((tm, d), lambda i: (i, 0)),
                  pl.BlockSpec((tm, LANES), lambda i: (i, 0)),
                  pl.BlockSpec((1, d), lambda i: (0, 0)),
                  pl.BlockSpec(memory_space=pl.ANY)],
        out_specs=pl.BlockSpec((tm, d), lambda i: (i, 0)),
        out_shape=jax.ShapeDtypeStruct((t, d), F32),
        scratch_shapes=[pltpu.VMEM((TOP_K, tm, w), U32), pltpu.SemaphoreType.DMA(())],
        compiler_params=_cparams(("arbitrary",), 40),
        name="moe_combine",
    )(dest.reshape(t // tm, 1, TOP_K * tm), x, gates, gain.reshape(1, d), ys)


def _rope_cols(w_rope):
    half = QK_ROPE // 2
    x1, x2 = w_rope[..., :half], w_rope[..., half:]
    return jnp.concatenate([x1, x2, x2, x1], axis=-1)


def _layer_weights(w_in, w_uq, w_ukv, w_router_g, b_router_g, w_router_e, b_router_e):
    d = w_in.shape[0]
    o = np.cumsum([0, Q_LORA, KV_LORA, QK_ROPE, HB * 2 * DH_B, HB * 2 * DH_B, HB * 2 * DH_B, d, d])
    w_cq, w_ckv, w_kr, w_qb, w_kb, w_vb, w_ga, w_gb = (w_in[:, o[n]:o[n + 1]] for n in range(8))
    pad = jnp.zeros((d, 2048 - Q_LORA - KV_LORA - LANES), w_in.dtype)
    w_lat = jnp.concatenate([w_cq, w_ckv, _rope_cols(w_kr), pad], axis=1).astype(BF16)
    w_qkb = jnp.concatenate([w_qb, w_kb], axis=1).astype(BF16)
    w_gate = jnp.concatenate([w_ga, w_gb], axis=1).astype(BF16)
    wq = w_uq.reshape(Q_LORA, HA, QK_NOPE + QK_ROPE)
    wq = jnp.concatenate([wq[..., :QK_NOPE], _rope_cols(wq[..., QK_NOPE:])], axis=-1).reshape(Q_LORA, HA * 2 * LANES)
    w_r = jnp.concatenate([w_router_e, w_router_g, jnp.zeros((d, LANES - N_EXPERTS - N_GROUPS), F32)], axis=1)
    w_r_hi = w_r.astype(BF16)
    w_r_lo = (w_r - w_r_hi.astype(F32)).astype(BF16)
    b_r = jnp.concatenate([b_router_e, b_router_g, jnp.zeros((LANES - N_EXPERTS - N_GROUPS,), F32)]).reshape(1, LANES)
    return w_lat, w_qkb, w_vb.astype(BF16), w_gate, wq.astype(BF16), w_ukv.astype(BF16), w_r_hi, w_r_lo, b_r


def kernel(x, positions, norm_attn, w_in, q_norm, w_uq, kv_norm, w_ukv, lam_q1, lam_k1, lam_q2, lam_k2, subln, w_oa,
           w_ob, w_out, norm_ffn, w_router_g, b_router_g, w_router_e, b_router_e, w1, w3, w2, norm_final):
    bsz, s, d = x.shape
    assert bsz == 1
    depth = w_in.shape[0]
    x = x.reshape(s, d)
    pos_col = positions.reshape(s, 1)
    table = _rope_table(pos_col)
    pos_col_f = pos_col.astype(F32)
    pos_row_f = pos_col_f.reshape(1, s)
    slopes_l2 = jnp.asarray(2.0 ** (-8.0 * (np.arange(HB) + 1) / HB) * LOG2E, dtype=F32)
    c_a = (QK_NOPE + QK_ROPE) ** -0.5 * LOG2E
    c_b = DH_B ** -0.5 * LOG2E
    tm = min(1024, s)
    att_t = min(ATT_T, s)
    hd = 2 * LANES

    def chunked_transpose(acc):
        t = jnp.concatenate([acc.T, jnp.ones((ONES_ROWS, tm), F32)], axis=0)
        return (jnp.stack([t[:, c * att_t:(c + 1) * att_t] for c in range(tm // att_t)]),)

    for l in range(depth):
        w_lat, w_qkb, w_vb, w_gate, wq, wkv, w_r_hi, w_r_lo, b_r = _layer_weights(
            w_in[l], w_uq[l], w_ukv[l], w_router_g[l], b_router_g[l], w_router_e[l], b_router_e[l])
        h = _rmsnorm(x, norm_attn[l], BF16)
        full = ((tm, d), lambda i, j: (i, 0))

        def ident(acc):
            return (acc,)

        (lat,) = _matmul("in_proj_latent", h, *full, w_lat, 512,
                         [(jax.ShapeDtypeStruct((s, 2048), BF16), (tm, 512), lambda i, j: (i, j))], ident)

        def scale_q(acc):
            return (acc * jnp.where(pl.program_id(1) < HB, c_b, 1.0),)

        (qk_b,) = _matmul("in_proj_qk_b", h, *full, w_qkb, hd,
                          [(jax.ShapeDtypeStruct((2 * HB, s, hd), BF16), (None, tm, hd), lambda i, j: (j, i, 0))],
                          scale_q)
        (vt_b,) = _matmul("in_proj_v_b", h, *full, w_vb, hd,
                          [(jax.ShapeDtypeStruct((HB, s // att_t, hd + ONES_ROWS, att_t), BF16),
                            (None, tm // att_t, hd + ONES_ROWS, att_t),
                            lambda i, j: (j, i, 0, 0))], chunked_transpose)
        (gates,) = _matmul("in_proj_gates", h, *full, w_gate, 512,
                           [(jax.ShapeDtypeStruct((s, 2 * d), BF16), (tm, 512), lambda i, j: (i, j))], ident)

        def q_epilogue(acc, t_ref):
            rot = _rope_128(acc[:, LANES:], t_ref[...])
            return (jnp.concatenate([acc[:, :LANES], rot], axis=1) * c_a,)

        (q_a,) = _matmul("q_up_proj", lat, (tm, Q_LORA), lambda i, j: (i, 0), wq, hd,
                         [(jax.ShapeDtypeStruct((HA, s, hd), BF16), (None, tm, hd), lambda i, j: (j, i, 0))],
                         q_epilogue, gain=q_norm[l], extras=[(table, (tm, LANES), lambda i, j: (i, 0))])

        def kv_epilogue(acc, kr_ref, t_ref):
            rot = _rope_128(kr_ref[...].astype(F32), t_ref[...])
            lane = lax.broadcasted_iota(I32, rot.shape, 1)
            rot = jnp.where(lane < QK_ROPE, rot, 0.0)
            return jnp.concatenate([acc[:, :LANES], rot], axis=1), chunked_transpose(acc[:, LANES:])[0]

        k_a, vt_a = _matmul("kv_up_proj", lat, (tm, KV_LORA), lambda i, j: (i, Q_LORA // KV_LORA), wkv, hd,
                            [(jax.ShapeDtypeStruct((HA, s, hd), BF16), (None, tm, hd), lambda i, j: (j, i, 0)),
                             (jax.ShapeDtypeStruct((HA, s // att_t, V_DIM_A + ONES_ROWS, att_t), BF16),
                              (None, tm // att_t, V_DIM_A + ONES_ROWS, att_t), lambda i, j: (j, i, 0, 0))],
                            kv_epilogue, gain=kv_norm[l],
                            extras=[(lat, (tm, LANES), lambda i, j: (i, (Q_LORA + KV_LORA) // LANES)),
                                    (table, (tm, LANES), lambda i, j: (i, 0))])
        o_a = _mla_attention(q_a, k_a, vt_a)

        lam_init = 0.8 - 0.6 * math.exp(-0.3 * l)
        lam_vecs = [v[l].reshape(1, DH_B).astype(F32) for v in (lam_q1, lam_k1, lam_q2, lam_k2)]
        o_b = _diff_attention(qk_b, vt_b, pos_col_f, pos_row_f, slopes_l2, lam_vecs, subln[l], lam_init)

        merged = _merge(o_a, o_b, w_oa[l].astype(BF16), w_ob[l].astype(BF16), gates)

        def residual(acc, x_ref):
            return (x_ref[...] + acc,)

        (x,) = _matmul("out_proj", merged, (tm, d), lambda i, j: (i, 0), w_out[l].astype(BF16), 512,
                       [(jax.ShapeDtypeStruct((s, d), F32), (tm, 512), lambda i, j: (i, j))], residual,
                       extras=[(x, (tm, 512), lambda i, j: (i, j))])

        ids, gate_vals, hp, counts = _router(x, norm_ffn[l], w_r_hi, w_r_lo, b_r)
        dest, blk_exp, nb_used, last_blk_row = _dispatch_plan(ids, counts, s)
        xs = _dispatch(hp, dest, last_blk_row, nb_used, blk_exp.shape[0])
        ys = _experts(xs, blk_exp, nb_used, w1[l].astype(BF16), w3[l].astype(BF16), w2[l].astype(BF16))
        final = l == depth - 1
        x = _combine(x, ys, dest, gate_vals, norm_final if final else norm_ffn[l], final)
    return x.reshape(bsz, s, d)
```

```python
import functools
import math

import jax
import jax.numpy as jnp
import numpy as np
from jax import lax
from jax.experimental import pallas as pl
from jax.experimental.pallas import tpu as pltpu

F32 = jnp.float32
BF16 = jnp.bfloat16
I32 = jnp.int32
U32 = jnp.uint32

EPS = 1e-6
ROPE_THETA = 10000.0
HA, QK_NOPE, QK_ROPE, V_DIM_A = 16, 128, 64, 128
Q_LORA, KV_LORA = 1024, 512
HB, DH_B = 8, 128
N_GROUPS, EXP_PER_GROUP, TOP_K = 8, 8, 2
N_EXPERTS = N_GROUPS * EXP_PER_GROUP
LOG2E = 1.4426950408889634
NEG = -1e30

LANES = 128
MOE_ROWS = 128
ATT_TK_A = 1024
ATT_TK_B = 512
ATT_TQ = 1024
ATT_UNROLL = 2
ONES_ROWS = 16
MIB = 1024 * 1024


def _cparams(semantics, vmem_mib):
    return pltpu.CompilerParams(dimension_semantics=semantics, vmem_limit_bytes=vmem_mib * MIB)


def _rms(xf, gain):
    ms = jnp.mean(xf * xf, axis=-1, keepdims=True)
    return xf * lax.rsqrt(ms + EPS) * gain


def _rmsnorm_body(x_ref, g_ref, o_ref):
    o_ref[...] = _rms(x_ref[...], g_ref[...]).astype(o_ref.dtype)


def _rmsnorm(x, gain, out_dtype, tm=256):
    s, d = x.shape
    tm = min(tm, s)
    return pl.pallas_call(
        _rmsnorm_body,
        grid=(s // tm,),
        in_specs=[pl.BlockSpec((tm, d), lambda i: (i, 0)), pl.BlockSpec((1, d), lambda i: (0, 0))],
        out_specs=pl.BlockSpec((tm, d), lambda i: (i, 0)),
        out_shape=jax.ShapeDtypeStruct((s, d), out_dtype),
        compiler_params=_cparams(("parallel",), 32),
        name="rmsnorm",
    )(x, gain.reshape(1, d))


def _matmul_body(*refs, n_extra, n_out, norm, epilogue):
    a_ref = refs[0]
    pos = 1
    if norm:
        g_ref = refs[pos]
        pos += 1
    b_ref = refs[pos]
    pos += 1
    extras = refs[pos:pos + n_extra]
    pos += n_extra
    outs = refs[pos:pos + n_out]
    pos += n_out
    if norm:
        lhs_ref = refs[pos]

        @pl.when(pl.program_id(1) == 0)
        def _():
            lhs_ref[...] = _rms(a_ref[...].astype(F32), g_ref[...]).astype(BF16)

        lhs = lhs_ref[...]
    else:
        lhs = a_ref[...]
    acc = jnp.dot(lhs, b_ref[...], preferred_element_type=F32)
    for o_ref, r in zip(outs, epilogue(acc, *extras)):
        o_ref[...] = r.astype(o_ref.dtype)


def _matmul(name, a, a_block, a_index, b, tn, outs, epilogue, gain=None, extras=(), vmem_mib=48):
    tm, k = a_block
    m = a.shape[0]
    n = b.shape[1]
    norm = gain is not None
    in_specs = [pl.BlockSpec(a_block, a_index)]
    args = [a]
    if norm:
        in_specs.append(pl.BlockSpec((1, k), lambda i, j: (0, 0)))
        args.append(gain.reshape(1, k).astype(F32))
    in_specs.append(pl.BlockSpec((k, tn), lambda i, j: (0, j)))
    args.append(b)
    for arr, blk, imap in extras:
        in_specs.append(pl.BlockSpec(blk, imap))
        args.append(arr)
    res = pl.pallas_call(
        functools.partial(_matmul_body, n_extra=len(extras), n_out=len(outs), norm=norm, epilogue=epilogue),
        grid=(m // tm, n // tn),
        in_specs=in_specs,
        out_specs=[pl.BlockSpec(blk, imap) for _, blk, imap in outs],
        out_shape=[sds for sds, _, _ in outs],
        scratch_shapes=[pltpu.VMEM((tm, k), BF16)] if norm else [],
        compiler_params=_cparams(("parallel", "arbitrary"), vmem_mib),
        name=name,
    )(*args)
    return res


def _rope_table_body(pos_ref, inv_ref, t_ref):
    ang = pos_ref[...].astype(F32) * inv_ref[...]
    c = jnp.cos(ang)
    s = jnp.sin(ang)
    lane = lax.broadcasted_iota(I32, ang.shape, 1)
    t_ref[...] = jnp.where(lane < 2 * (QK_ROPE // 2), c, jnp.where(lane < 3 * (QK_ROPE // 2), -s, s))


def _rope_table(positions_col, tm=512):
    s = positions_col.shape[0]
    tm = min(tm, s)
    half = QK_ROPE // 2
    inv = ROPE_THETA ** (-jnp.arange(half, dtype=F32) / half)
    inv4 = jnp.tile(inv, 4).reshape(1, LANES)
    return pl.pallas_call(
        _rope_table_body,
        grid=(s // tm,),
        in_specs=[pl.BlockSpec((tm, 1), lambda i: (i, 0)), pl.BlockSpec((1, LANES), lambda i: (0, 0))],
        out_specs=pl.BlockSpec((tm, LANES), lambda i: (i, 0)),
        out_shape=jax.ShapeDtypeStruct((s, LANES), F32),
        compiler_params=_cparams(("parallel",), 16),
        name="rope_table",
    )(positions_col, inv4)


def _rope_128(v, table):
    t = v * table
    return t + pltpu.roll(t, 2 * (QK_ROPE // 2), axis=1)


def _softmax_step_t(st, vt, m_ref, acc_ref):
    m_prev = m_ref[...]
    m_new = jnp.maximum(m_prev, jnp.max(st, axis=0, keepdims=True))
    alpha = jnp.exp2(m_prev - m_new)
    p = jnp.exp2(st - m_new)
    acc_ref[...] = acc_ref[...] * alpha + jnp.dot(vt, p.astype(BF16), preferred_element_type=F32)
    m_ref[...] = m_new


def _causal_bias(tk, tq):
    key = jnp.arange(tq, dtype=I32).reshape(tq // tk, tk, 1)
    qry = jnp.arange(tq, dtype=I32).reshape(1, 1, tq)
    return jnp.where(key <= qry, 0.0, NEG).astype(F32)


def _init_stats(m_ref, acc_ref):
    m_ref[...] = jnp.full(m_ref.shape, NEG, F32)
    acc_ref[...] = jnp.zeros(acc_ref.shape, F32)


def _normalized(acc_ref, chain, dv):
    return acc_ref[chain, :dv, :] / acc_ref[chain, dv:dv + 1, :]


def _causal_pipeline(n_common, n_diag, chains, scores, update, s_ref, unroll=ATT_UNROLL):
    assert unroll % 2 == 0

    def ahead(c, slot):
        for a, st in zip(chains, scores(c)):
            s_ref[2 * a + slot] = st

    def fold(c, slot, d=None):
        for a in chains:
            update(c, a, s_ref[2 * a + slot], d)

    def step(c, t):
        ahead(c + 1, (t + 1) % 2)
        fold(c, t % 2)

    ahead(0, 0)
    n_groups = n_common // unroll

    def body(g, carry):
        for t in range(unroll):
            step(g * unroll + t, t)
        return carry

    lax.fori_loop(0, n_groups, body, 0)
    c0 = n_groups * unroll
    rem = n_common - c0
    for t in range(unroll - 1):
        @pl.when(t < rem)
        def _(t=t):
            step(c0 + t, t)

    for parity in range(2):
        @pl.when(rem % 2 == parity)
        def _(parity=parity):
            for d in range(n_diag):
                if d + 1 < n_diag:
                    ahead(n_common + d + 1, (parity + d + 1) % 2)
                fold(n_common + d, (parity + d) % 2, d)


def _mla_body(qt_ref, k_ref, vt_ref, mask_ref, o_ref, m_ref, acc_ref, s_ref, *, tq, tk):
    i = pl.program_id(1)
    _init_stats(m_ref, acc_ref)
    qt = qt_ref[...]

    def scores(c):
        return [jnp.dot(k_ref[pl.ds(pl.multiple_of(c * tk, tk), tk), :], qt, preferred_element_type=F32)]

    def update(c, a, st, d):
        if d is not None:
            st = st + mask_ref[d]
        _softmax_step_t(st, vt_ref[c], m_ref.at[a], acc_ref.at[a])

    _causal_pipeline(i * (tq // tk), tq // tk, [0], scores, update, s_ref)
    o_ref[...] = _normalized(acc_ref, 0, o_ref.shape[1]).T.astype(o_ref.dtype)


def _mla_attention(qt, k, vt):
    h, s, dk = k.shape
    _, nchunk, dvp, tk = vt.shape
    dv = dvp - ONES_ROWS
    tq = max(min(ATT_TQ, s), tk)
    return pl.pallas_call(
        functools.partial(_mla_body, tq=tq, tk=tk),
        grid=(h, s // tq),
        in_specs=[
            pl.BlockSpec((None, dk, tq), lambda hh, i: (hh, 0, i)),
            pl.BlockSpec((None, s, dk), lambda hh, i: (hh, 0, 0)),
            pl.BlockSpec((None, nchunk, dvp, tk), lambda hh, i: (hh, 0, 0, 0)),
            pl.BlockSpec((tq // tk, tk, tq), lambda hh, i: (0, 0, 0), pipeline_mode=pl.Buffered(1)),
        ],
        out_specs=pl.BlockSpec((tq, dv), lambda hh, i: (i, hh)),
        out_shape=jax.ShapeDtypeStruct((s, h * dv), BF16),
        scratch_shapes=[pltpu.VMEM((1, 1, tq), F32), pltpu.VMEM((1, dvp, tq), F32), pltpu.VMEM((2, tk, tq), F32)],
        compiler_params=_cparams(("parallel", "arbitrary"), 48),
        name="mla_attention",
    )(qt, k, vt, _causal_bias(tk, tq))


def _diff_body(slope_ref, qt_ref, k_ref, vt_ref, mask_ref, pcol_ref, prow_ref, lq1_ref, lk1_ref, lq2_ref, lk2_ref,
               sub_ref, o_ref, m_ref, acc_ref, s_ref, *, tq, tk, lam_init):
    h = pl.program_id(0)
    i = pl.program_id(1)
    _init_stats(m_ref, acc_ref)
    qts = [qt_ref[c * DH_B:(c + 1) * DH_B, :] for c in range(2)]
    pq = prow_ref[...]
    slope = slope_ref[h]

    def scores(c):
        start = pl.multiple_of(c * tk, tk)
        k = k_ref[pl.ds(start, tk), :]
        bias = jnp.abs(pcol_ref[pl.ds(start, tk), :] - pq) * slope
        return [jnp.dot(k[:, mp * DH_B:(mp + 1) * DH_B], qts[mp], preferred_element_type=F32) - bias
                for mp in range(2)]

    def update(c, chain, st, d):
        if d is not None:
            st = st + mask_ref[d]
        _softmax_step_t(st, vt_ref[c], m_ref.at[chain], acc_ref.at[chain])

    _causal_pipeline(i * (tq // tk), tq // tk, [0, 1], scores, update, s_ref)

    lam = (jnp.exp(jnp.sum(lq1_ref[...] * lk1_ref[...], axis=1, keepdims=True))
           - jnp.exp(jnp.sum(lq2_ref[...] * lk2_ref[...], axis=1, keepdims=True)) + lam_init)
    dv = o_ref.shape[1]
    o = _normalized(acc_ref, 0, dv) - lam * _normalized(acc_ref, 1, dv)
    ms = jnp.mean(o * o, axis=0, keepdims=True)
    y = o * lax.rsqrt(ms + EPS) * sub_ref[...] * (1.0 - lam_init)
    o_ref[...] = y.T.astype(o_ref.dtype)


def _diff_attention(qt, k, vt, pos_col, pos_row, slopes_l2, lam_vecs, subln, lam_init):
    _, s, dqk = k.shape
    _, nchunk, dvp, tk = vt.shape
    dv = dvp - ONES_ROWS
    tq = max(min(ATT_TQ, s), tk)
    vec = pl.BlockSpec((1, DH_B), lambda hh, i: (0, 0))
    once = pl.Buffered(1)
    return pl.pallas_call(
        functools.partial(_diff_body, tq=tq, tk=tk, lam_init=lam_init),
        grid=(HB, s // tq),
        in_specs=[
            pl.BlockSpec(memory_space=pltpu.SMEM),
            pl.BlockSpec((None, dqk, tq), lambda hh, i: (hh, 0, i)),
            pl.BlockSpec((None, s, dqk), lambda hh, i: (hh, 0, 0), pipeline_mode=once),
            pl.BlockSpec((None, nchunk, dvp, tk), lambda hh, i: (hh, 0, 0, 0), pipeline_mode=once),
            pl.BlockSpec((tq // tk, tk, tq), lambda hh, i: (0, 0, 0), pipeline_mode=once),
            pl.BlockSpec((s, 1), lambda hh, i: (0, 0), pipeline_mode=once),
            pl.BlockSpec((1, tq), lambda hh, i: (0, i)),
            vec, vec, vec, vec,
            pl.BlockSpec((dv, 1), lambda hh, i: (0, 0)),
        ],
        out_specs=pl.BlockSpec((tq, dv), lambda hh, i: (i, hh)),
        out_shape=jax.ShapeDtypeStruct((s, HB * dv), BF16),
        scratch_shapes=[pltpu.VMEM((2, 1, tq), F32), pltpu.VMEM((2, dvp, tq), F32), pltpu.VMEM((4, tk, tq), F32)],
        compiler_params=_cparams(("parallel", "arbitrary"), 56),
        name="diff_attention",
    )(slopes_l2, qt, k, vt, _causal_bias(tk, tq), pos_col, pos_row, *lam_vecs, subln.reshape(dv, 1))


def _merge_body(oa_ref, ob_ref, woa_ref, wob_ref, ga_ref, gb_ref, o_ref):
    ya = jnp.dot(oa_ref[...], woa_ref[...], preferred_element_type=F32)
    yb = jnp.dot(ob_ref[...], wob_ref[...], preferred_element_type=F32)
    out = jax.nn.sigmoid(ga_ref[...].astype(F32)) * ya + jax.nn.sigmoid(gb_ref[...].astype(F32)) * yb
    o_ref[...] = out.astype(o_ref.dtype)


def _merge(o_a, o_b, w_oa, w_ob, gates, tm=1024, tn=512):
    s, ka = o_a.shape
    kb = o_b.shape[1]
    d = w_oa.shape[1]
    tm = min(tm, s)
    nj = d // tn
    return pl.pallas_call(
        _merge_body,
        grid=(s // tm, nj),
        in_specs=[
            pl.BlockSpec((tm, ka), lambda i, j: (i, 0)),
            pl.BlockSpec((tm, kb), lambda i, j: (i, 0)),
            pl.BlockSpec((ka, tn), lambda i, j: (0, j)),
            pl.BlockSpec((kb, tn), lambda i, j: (0, j)),
            pl.BlockSpec((tm, tn), lambda i, j: (i, j)),
            pl.BlockSpec((tm, tn), lambda i, j: (i, nj + j)),
        ],
        out_specs=pl.BlockSpec((tm, tn), lambda i, j: (i, j)),
        out_shape=jax.ShapeDtypeStruct((s, d), BF16),
        compiler_params=_cparams(("parallel", "arbitrary"), 48),
        name="gated_merge",
    )(o_a, o_b, w_oa, w_ob, gates, gates)


def _pack_pairs(x):
    half = x.shape[1] // 2
    lo = pltpu.bitcast(x[:, :half].astype(BF16).astype(F32), U32)
    hi = pltpu.bitcast(x[:, half:].astype(BF16).astype(F32), U32)
    return (lo >> 16) | (hi & jnp.uint32(0xFFFF0000))


def _unpack_pairs(p):
    lo = pltpu.bitcast(p << 16, F32)
    hi = pltpu.bitcast(p & jnp.uint32(0xFFFF0000), F32)
    return jnp.concatenate([lo, hi], axis=1)


def _router_body(x_ref, g_ref, whi_ref, wlo_ref, b_ref, tri_ref, ids_ref, gates_ref, hp_ref, cnt_ref):
    @pl.when(pl.program_id(0) == 0)
    def _():
        cnt_ref[...] = jnp.zeros(cnt_ref.shape, F32)

    h = _rms(x_ref[...], g_ref[...])
    hp_ref[...] = _pack_pairs(h)
    h_hi = h.astype(BF16)
    h_lo = (h - h_hi.astype(F32)).astype(BF16)
    logits = (jnp.dot(h_hi, whi_ref[...], preferred_element_type=F32)
              + jnp.dot(h_lo, whi_ref[...], preferred_element_type=F32)
              + jnp.dot(h_hi, wlo_ref[...], preferred_element_type=F32)) + b_ref[...]
    lane = lax.broadcasted_iota(I32, logits.shape, 1).astype(F32)

    def first_argmax(vals, mx):
        return jnp.min(jnp.where(vals == mx, lane, 4.0 * LANES), axis=1, keepdims=True)

    is_g = jnp.logical_and(lane >= N_EXPERTS, lane < N_EXPERTS + N_GROUPS)
    lg = jnp.where(is_g, logits, NEG)
    mg = jnp.max(lg, axis=1, keepdims=True)
    p_g = 1.0 / jnp.sum(jnp.exp(lg - mg), axis=1, keepdims=True)
    g_idx = first_argmax(lg, mg) - N_EXPERTS
    lo = g_idx * EXP_PER_GROUP
    in_grp = jnp.logical_and(lane >= lo, lane < lo + EXP_PER_GROUP)
    le = jnp.where(in_grp, logits, NEG)
    m1 = jnp.max(le, axis=1, keepdims=True)
    i1 = first_argmax(le, m1)
    den = jnp.sum(jnp.exp(le - m1), axis=1, keepdims=True)
    le2 = jnp.where(lane == i1, NEG, le)
    m2 = jnp.max(le2, axis=1, keepdims=True)
    i2 = first_argmax(le2, m2)
    p1 = 1.0 / den
    p2 = jnp.exp(m2 - m1) / den
    psum = p1 + p2
    gate1 = p_g * p1 / psum
    gate2 = p_g * p2 / psum

    hot1 = jnp.where(lane == i1, 1.0, 0.0)
    hot2 = jnp.where(lane == i2, 1.0, 0.0)
    tri = tri_ref[...]
    before1 = jnp.dot(tri, hot1.astype(BF16), preferred_element_type=F32) + cnt_ref[...]
    tot1 = jnp.sum(hot1, axis=0, keepdims=True)
    before2 = jnp.dot(tri, hot2.astype(BF16), preferred_element_type=F32) + (cnt_ref[...] + tot1)
    r1 = jnp.sum(hot1 * before1, axis=1, keepdims=True)
    r2 = jnp.sum(hot2 * before2, axis=1, keepdims=True)
    cnt_ref[...] = cnt_ref[...] + tot1 + jnp.sum(hot2, axis=0, keepdims=True)

    ids = jnp.where(lane == 0, i1, jnp.where(lane == 1, i2, jnp.where(lane == 2, r1, jnp.where(lane == 3, r2, 0.0))))
    ids_ref[...] = ids.astype(I32)
    gates_ref[...] = jnp.where(lane == 0, gate1, jnp.where(lane == 1, gate2, 0.0))


def _router(x, gain, w_hi, w_lo, bias, tm=256):
    t, d = x.shape
    tm = min(tm, t)
    tri = jnp.tril(jnp.ones((tm, tm), F32), -1).astype(BF16)
    return pl.pallas_call(
        _router_body,
        grid=(t // tm,),
        in_specs=[
            pl.BlockSpec((tm, d), lambda i: (i, 0)),
            pl.BlockSpec((1, d), lambda i: (0, 0)),
            pl.BlockSpec((d, LANES), lambda i: (0, 0)),
            pl.BlockSpec((d, LANES), lambda i: (0, 0)),
            pl.BlockSpec((1, LANES), lambda i: (0, 0)),
            pl.BlockSpec((tm, tm), lambda i: (0, 0)),
        ],
        out_specs=[pl.BlockSpec((tm, LANES), lambda i: (i, 0)), pl.BlockSpec((tm, LANES), lambda i: (i, 0)),
                   pl.BlockSpec((tm, d // 2), lambda i: (i, 0)), pl.BlockSpec((1, LANES), lambda i: (0, 0))],
        out_shape=[jax.ShapeDtypeStruct((t, LANES), I32), jax.ShapeDtypeStruct((t, LANES), F32),
                   jax.ShapeDtypeStruct((t, d // 2), U32), jax.ShapeDtypeStruct((1, LANES), F32)],
        compiler_params=_cparams(("arbitrary",), 40),
        name="moe_router",
    )(x, gain.reshape(1, d), w_hi, w_lo, bias, tri)


def _dispatch_plan(ids, counts, t):
    rows = MOE_ROWS
    nb = t * TOP_K // rows + N_EXPERTS
    e = ids[:, :TOP_K]
    rank = ids[:, TOP_K:2 * TOP_K]
    cnt = counts[0, :N_EXPERTS].astype(I32)
    padded = (cnt + rows - 1) // rows * rows
    experts = jnp.arange(N_EXPERTS, dtype=I32)
    pad_end = jnp.sum(jnp.where(experts[None, :] <= experts[:, None], padded[None, :], 0), axis=1).astype(I32)
    pad_start = pad_end - padded
    hot = e[:, :, None] == experts
    dest = jnp.sum(jnp.where(hot, pad_start, 0), axis=-1).astype(I32) + rank
    nb_used = (pad_end[-1:] // rows).astype(I32)
    blk_row = jnp.arange(nb, dtype=I32) * rows
    blk_exp = jnp.minimum(jnp.sum(pad_end[None, :] <= blk_row[:, None], axis=1), N_EXPERTS - 1).astype(I32)
    last_blk_row = jnp.where(cnt > 0, pad_end - rows, -1).astype(I32)
    return dest, blk_exp, nb_used, last_blk_row


def _dispatch_body(last_ref, nbu_ref, dest_ref, hp_ref, xs_hbm, zbuf, sem_z, sem, *, tm, rows, nb):
    i = pl.program_id(0)

    @pl.when(i == 0)
    def _():
        zbuf[...] = jnp.zeros(zbuf.shape, U32)

        def expert_fill(e):
            return pltpu.make_async_copy(zbuf, xs_hbm.at[pl.ds(pl.multiple_of(last_ref[e], rows), rows)], sem_z)

        def tail_fill(b):
            return pltpu.make_async_copy(zbuf, xs_hbm.at[pl.ds(pl.multiple_of(b * rows, rows), rows)], sem_z)

        for wait in (False, True):
            def experts(e, c, wait=wait):
                @pl.when(last_ref[e] >= 0)
                def _():
                    expert_fill(e).wait() if wait else expert_fill(e).start()
                return c

            def tail(b, c, wait=wait):
                @pl.when(b >= nbu_ref[0])
                def _():
                    tail_fill(b).wait() if wait else tail_fill(b).start()
                return c

            lax.fori_loop(0, N_EXPERTS, experts, 0)
            lax.fori_loop(0, nb, tail, 0)

    def row_copy(r, k):
        return pltpu.make_async_copy(hp_ref.at[pl.ds(r, 1)], xs_hbm.at[pl.ds(dest_ref[0, TOP_K * r + k], 1)], sem)

    for wait in (False, True):
        def rows_loop(r, c, wait=wait):
            for k in range(TOP_K):
                row_copy(r, k).wait() if wait else row_copy(r, k).start()
            return c

        lax.fori_loop(0, tm, rows_loop, 0)


def _dispatch(hp, dest, last_blk_row, nb_used, nb, tm=256):
    t, w = hp.shape
    tm = min(tm, t)
    rows = MOE_ROWS
    grid_spec = pltpu.PrefetchScalarGridSpec(
        num_scalar_prefetch=2,
        grid=(t // tm,),
        in_specs=[pl.BlockSpec((None, 1, TOP_K * tm), lambda i, lr, nu: (i, 0, 0), memory_space=pltpu.SMEM),
                  pl.BlockSpec((tm, w), lambda i, lr, nu: (i, 0))],
        out_specs=pl.BlockSpec(memory_space=pl.ANY),
        scratch_shapes=[pltpu.VMEM((rows, w), U32), pltpu.SemaphoreType.DMA(()), pltpu.SemaphoreType.DMA(())],
    )
    return pl.pallas_call(
        functools.partial(_dispatch_body, tm=tm, rows=rows, nb=nb),
        grid_spec=grid_spec,
        out_shape=jax.ShapeDtypeStruct((nb * rows, w), U32),
        compiler_params=_cparams(("arbitrary",), 16),
        name="moe_dispatch",
    )(last_blk_row, nb_used, dest.reshape(t // tm, 1, TOP_K * tm), hp)


def _expert_body(blk_exp_ref, nbu_ref, xs_ref, w1_ref, w3_ref, w2_ref, ys_ref):
    del blk_exp_ref
    b = pl.program_id(0)

    @pl.when(b < nbu_ref[0])
    def _():
        xn = _unpack_pairs(xs_ref[...]).astype(BF16)
        a1 = jnp.dot(xn, w1_ref[...], preferred_element_type=F32)
        a3 = jnp.dot(xn, w3_ref[...], preferred_element_type=F32)
        hmid = (jax.nn.silu(a1) * a3).astype(BF16)
        ys_ref[...] = _pack_pairs(jnp.dot(hmid, w2_ref[...], preferred_element_type=F32))

    @pl.when(b >= nbu_ref[0])
    def _():
        ys_ref[...] = jnp.zeros(ys_ref.shape, U32)


def _experts(xs, blk_exp, nb_used, w1, w3, w2):
    p, w = xs.shape
    _, d, de = w1.shape
    rows = MOE_ROWS
    grid_spec = pltpu.PrefetchScalarGridSpec(
        num_scalar_prefetch=2,
        grid=(p // rows,),
        in_specs=[
            pl.BlockSpec((rows, w), lambda b, be, nu: (jnp.minimum(b, nu[0] - 1), 0)),
            pl.BlockSpec((None, d, de), lambda b, be, nu: (be[b], 0, 0)),
            pl.BlockSpec((None, d, de), lambda b, be, nu: (be[b], 0, 0)),
            pl.BlockSpec((None, de, d), lambda b, be, nu: (be[b], 0, 0)),
        ],
        out_specs=pl.BlockSpec((rows, w), lambda b, be, nu: (b, 0)),
    )
    return pl.pallas_call(
        _expert_body,
        grid_spec=grid_spec,
        out_shape=jax.ShapeDtypeStruct((p, w), U32),
        compiler_params=_cparams(("arbitrary",), 40),
        name="moe_experts",
    )(blk_exp, nb_used, xs, w1, w3, w2)


def _combine_body(dest_ref, x_ref, gates_ref, g_ref, ys_hbm, o_ref, ybuf, sem, *, tm, final):
    def row_copy(r, k):
        return pltpu.make_async_copy(ys_hbm.at[pl.ds(dest_ref[0, TOP_K * r + k], 1)], ybuf.at[k, pl.ds(r, 1)], sem)

    for wait in (False, True):
        def rows_loop(r, c, wait=wait):
            for k in range(TOP_K):
                row_copy(r, k).wait() if wait else row_copy(r, k).start()
            return c

        lax.fori_loop(0, tm, rows_loop, 0)

    gates = gates_ref[...]
    x = x_ref[...] + (gates[:, 0:1] * _unpack_pairs(ybuf[0]) + gates[:, 1:2] * _unpack_pairs(ybuf[1]))
    o_ref[...] = _rms(x, g_ref[...]) if final else x


def _combine(x, ys, dest, gates, gain, final, tm=256):
    t, d = x.shape
    tm = min(tm, t)
    w = ys.shape[1]
    return pl.pallas_call(
        functools.partial(_combine_body, tm=tm, final=final),
        grid=(t // tm,),
        in_specs=[pl.BlockSpec((None, 1, TOP_K * tm), lambda i: (i, 0, 0), memory_space=pltpu.SMEM),
                  pl.BlockSpec((tm, d), lambda i: (i, 0)),
                  pl.BlockSpec((tm, LANES), lambda i: (i, 0)),
                  pl.BlockSpec((1, d), lambda i: (0, 0)),
                  pl.BlockSpec(memory_space=pl.ANY)],
        out_specs=pl.BlockSpec((tm, d), lambda i: (i, 0)),
        out_shape=jax.ShapeDtypeStruct((t, d), F32),
        scratch_shapes=[pltpu.VMEM((TOP_K, tm, w), U32), pltpu.SemaphoreType.DMA(())],
        compiler_params=_cparams(("arbitrary",), 40),
        name="moe_combine",
    )(dest.reshape(t // tm, 1, TOP_K * tm), x, gates, gain.reshape(1, d), ys)


def _rope_cols(w_rope):
    half = QK_ROPE // 2
    x1, x2 = w_rope[..., :half], w_rope[..., half:]
    return jnp.concatenate([x1, x2, x2, x1], axis=-1)


def _layer_weights(w_in, w_uq, w_ukv, w_router_g, b_router_g, w_router_e, b_router_e):
    d = w_in.shape[0]
    o = np.cumsum([0, Q_LORA, KV_LORA, QK_ROPE, HB * 2 * DH_B, HB * 2 * DH_B, HB * 2 * DH_B, d, d])
    w_cq, w_ckv, w_kr, w_qb, w_kb, w_vb, w_ga, w_gb = (w_in[:, o[n]:o[n + 1]] for n in range(8))
    pad = jnp.zeros((d, 2048 - Q_LORA - KV_LORA - LANES), w_in.dtype)
    w_lat = jnp.concatenate([w_cq, w_ckv, _rope_cols(w_kr), pad], axis=1).astype(BF16)
    w_gate = jnp.concatenate([w_ga, w_gb], axis=1).astype(BF16)
    wq = w_uq.reshape(Q_LORA, HA, QK_NOPE + QK_ROPE)
    wq = jnp.concatenate([wq[..., :QK_NOPE], _rope_cols(wq[..., QK_NOPE:])], axis=-1).reshape(Q_LORA, HA * 2 * LANES)
    w_r = jnp.concatenate([w_router_e, w_router_g, jnp.zeros((d, LANES - N_EXPERTS - N_GROUPS), F32)], axis=1)
    w_r_hi = w_r.astype(BF16)
    w_r_lo = (w_r - w_r_hi.astype(F32)).astype(BF16)
    b_r = jnp.concatenate([b_router_e, b_router_g, jnp.zeros((LANES - N_EXPERTS - N_GROUPS,), F32)]).reshape(1, LANES)
    return w_lat, w_qb.astype(BF16), w_kb.astype(BF16), w_vb.astype(BF16), w_gate, wq.astype(BF16), w_ukv.astype(BF16), w_r_hi, w_r_lo, b_r


def kernel(x, positions, norm_attn, w_in, q_norm, w_uq, kv_norm, w_ukv, lam_q1, lam_k1, lam_q2, lam_k2, subln, w_oa,
           w_ob, w_out, norm_ffn, w_router_g, b_router_g, w_router_e, b_router_e, w1, w3, w2, norm_final):
    bsz, s, d = x.shape
    assert bsz == 1
    depth = w_in.shape[0]
    x = x.reshape(s, d)
    pos_col = positions.reshape(s, 1)
    table = _rope_table(pos_col)
    pos_col_f = pos_col.astype(F32)
    pos_row_f = pos_col_f.reshape(1, s)
    slopes_l2 = jnp.asarray(2.0 ** (-8.0 * (np.arange(HB) + 1) / HB) * LOG2E, dtype=F32)
    c_a = (QK_NOPE + QK_ROPE) ** -0.5 * LOG2E
    c_b = DH_B ** -0.5 * LOG2E
    tm = min(1024, s)
    tk_a, tk_b = min(ATT_TK_A, s), min(ATT_TK_B, s)
    hd = 2 * LANES

    def chunked_transpose(acc, tk):
        t = jnp.concatenate([acc.T, jnp.ones((ONES_ROWS, tm), F32)], axis=0)
        return jnp.stack([t[:, c * tk:(c + 1) * tk] for c in range(tm // tk)])

    for l in range(depth):
        w_lat, w_qb, w_kb, w_vb, w_gate, wq, wkv, w_r_hi, w_r_lo, b_r = _layer_weights(
            w_in[l], w_uq[l], w_ukv[l], w_router_g[l], b_router_g[l], w_router_e[l], b_router_e[l])
        h = _rmsnorm(x, norm_attn[l], BF16)
        full = ((tm, d), lambda i, j: (i, 0))

        def ident(acc):
            return (acc,)

        (lat,) = _matmul("in_proj_latent", h, *full, w_lat, 512,
                         [(jax.ShapeDtypeStruct((s, 2048), BF16), (tm, 512), lambda i, j: (i, j))], ident)

        (qt_b,) = _matmul("in_proj_q_b", h, *full, w_qb, hd,
                          [(jax.ShapeDtypeStruct((HB, hd, s), BF16), (None, hd, tm), lambda i, j: (j, 0, i))],
                          lambda acc: ((acc * c_b).T,))
        (k_b,) = _matmul("in_proj_k_b", h, *full, w_kb, hd,
                         [(jax.ShapeDtypeStruct((HB, s, hd), BF16), (None, tm, hd), lambda i, j: (j, i, 0))], ident)
        (vt_b,) = _matmul("in_proj_v_b", h, *full, w_vb, hd,
                          [(jax.ShapeDtypeStruct((HB, s // tk_b, hd + ONES_ROWS, tk_b), BF16),
                            (None, tm // tk_b, hd + ONES_ROWS, tk_b),
                            lambda i, j: (j, i, 0, 0))], lambda acc: (chunked_transpose(acc, tk_b),))
        (gates,) = _matmul("in_proj_gates", h, *full, w_gate, 512,
                           [(jax.ShapeDtypeStruct((s, 2 * d), BF16), (tm, 512), lambda i, j: (i, j))], ident)

        def q_epilogue(acc, t_ref):
            rot = _rope_128(acc[:, LANES:], t_ref[...])
            return ((jnp.concatenate([acc[:, :LANES], rot], axis=1) * c_a).T,)

        (qt_a,) = _matmul("q_up_proj", lat, (tm, Q_LORA), lambda i, j: (i, 0), wq, hd,
                          [(jax.ShapeDtypeStruct((HA, hd, s), BF16), (None, hd, tm), lambda i, j: (j, 0, i))],
                          q_epilogue, gain=q_norm[l], extras=[(table, (tm, LANES), lambda i, j: (i, 0))])

        def kv_epilogue(acc, kr_ref, t_ref):
            rot = _rope_128(kr_ref[...].astype(F32), t_ref[...])
            lane = lax.broadcasted_iota(I32, rot.shape, 1)
            rot = jnp.where(lane < QK_ROPE, rot, 0.0)
            return jnp.concatenate([acc[:, :LANES], rot], axis=1), chunked_transpose(acc[:, LANES:], tk_a)

        k_a, vt_a = _matmul("kv_up_proj", lat, (tm, KV_LORA), lambda i, j: (i, Q_LORA // KV_LORA), wkv, hd,
                            [(jax.ShapeDtypeStruct((HA, s, hd), BF16), (None, tm, hd), lambda i, j: (j, i, 0)),
                             (jax.ShapeDtypeStruct((HA, s // tk_a, V_DIM_A + ONES_ROWS, tk_a), BF16),
                              (None, tm // tk_a, V_DIM_A + ONES_ROWS, tk_a), lambda i, j: (j, i, 0, 0))],
                            kv_epilogue, gain=kv_norm[l],
                            extras=[(lat, (tm, LANES), lambda i, j: (i, (Q_LORA + KV_LORA) // LANES)),
                                    (table, (tm, LANES), lambda i, j: (i, 0))])
        o_a = _mla_attention(qt_a, k_a, vt_a)

        lam_init = 0.8 - 0.6 * math.exp(-0.3 * l)
        lam_vecs = [v[l].reshape(1, DH_B).astype(F32) for v in (lam_q1, lam_k1, lam_q2, lam_k2)]
        o_b = _diff_attention(qt_b, k_b, vt_b, pos_col_f, pos_row_f, slopes_l2, lam_vecs, subln[l], lam_init)

        merged = _merge(o_a, o_b, w_oa[l].astype(BF16), w_ob[l].astype(BF16), gates)

        def residual(acc, x_ref):
            return (x_ref[...] + acc,)

        (x,) = _matmul("out_proj", merged, (tm, d), lambda i, j: (i, 0), w_out[l].astype(BF16), 512,
                       [(jax.ShapeDtypeStruct((s, d), F32), (tm, 512), lambda i, j: (i, j))], residual,
                       extras=[(x, (tm, 512), lambda i, j: (i, j))])

        ids, gate_vals, hp, counts = _router(x, norm_ffn[l], w_r_hi, w_r_lo, b_r)
        dest, blk_exp, nb_used, last_blk_row = _dispatch_plan(ids, counts, s)
        xs = _dispatch(hp, dest, last_blk_row, nb_used, blk_exp.shape[0])
        ys = _experts(xs, blk_exp, nb_used, w1[l].astype(BF16), w3[l].astype(BF16), w2[l].astype(BF16))
        final = l == depth - 1
        x = _combine(x, ys, dest, gate_vals, norm_final if final else norm_ffn[l], final)
    return x.reshape(bsz, s, d)
```

```python
import functools
import math

import jax
import jax.numpy as jnp
import numpy as np
from jax import lax
from jax.experimental import pallas as pl
from jax.experimental.pallas import tpu as pltpu

F32 = jnp.float32
BF16 = jnp.bfloat16
I32 = jnp.int32
U32 = jnp.uint32

EPS = 1e-6
ROPE_THETA = 10000.0
HA, QK_NOPE, QK_ROPE, V_DIM_A = 16, 128, 64, 128
Q_LORA, KV_LORA = 1024, 512
HB, DH_B = 8, 128
N_GROUPS, EXP_PER_GROUP, TOP_K = 8, 8, 2
N_EXPERTS = N_GROUPS * EXP_PER_GROUP
LOG2E = 1.4426950408889634
NEG = -1e30

LANES = 128
MOE_ROWS = 128
ATT_TK_A = 1024
ATT_TK_B = 512
ATT_TQ = 1024
ATT_UNROLL = 2
ONES_ROWS = 16
MIB = 1024 * 1024


def _cparams(semantics, vmem_mib):
    return pltpu.CompilerParams(dimension_semantics=semantics, vmem_limit_bytes=vmem_mib * MIB)


def _rms(xf, gain):
    ms = jnp.mean(xf * xf, axis=-1, keepdims=True)
    return xf * lax.rsqrt(ms + EPS) * gain


def _rmsnorm_body(x_ref, g_ref, o_ref):
    o_ref[...] = _rms(x_ref[...], g_ref[...]).astype(o_ref.dtype)


def _rmsnorm(x, gain, out_dtype, tm=256):
    s, d = x.shape
    tm = min(tm, s)
    return pl.pallas_call(
        _rmsnorm_body,
        grid=(s // tm,),
        in_specs=[pl.BlockSpec((tm, d), lambda i: (i, 0)), pl.BlockSpec((1, d), lambda i: (0, 0))],
        out_specs=pl.BlockSpec((tm, d), lambda i: (i, 0)),
        out_shape=jax.ShapeDtypeStruct((s, d), out_dtype),
        compiler_params=_cparams(("parallel",), 32),
        name="rmsnorm",
    )(x, gain.reshape(1, d))


def _matmul_body(*refs, n_extra, n_out, norm, epilogue):
    a_ref = refs[0]
    pos = 1
    if norm:
        g_ref = refs[pos]
        pos += 1
    b_ref = refs[pos]
    pos += 1
    extras = refs[pos:pos + n_extra]
    pos += n_extra
    outs = refs[pos:pos + n_out]
    pos += n_out
    if norm:
        lhs_ref = refs[pos]

        @pl.when(pl.program_id(1) == 0)
        def _():
            lhs_ref[...] = _rms(a_ref[...].astype(F32), g_ref[...]).astype(BF16)

        lhs = lhs_ref[...]
    else:
        lhs = a_ref[...]
    acc = jnp.dot(lhs, b_ref[...], preferred_element_type=F32)
    for o_ref, r in zip(outs, epilogue(acc, *extras)):
        o_ref[...] = r.astype(o_ref.dtype)


def _matmul(name, a, a_block, a_index, b, tn, outs, epilogue, gain=None, extras=(), vmem_mib=48, b_layer=None):
    tm, k = a_block
    m = a.shape[0]
    n = b.shape[-1]
    norm = gain is not None
    in_specs = [pl.BlockSpec(a_block, a_index)]
    args = [a]
    if norm:
        in_specs.append(pl.BlockSpec((1, k), lambda i, j: (0, 0)))
        args.append(gain.reshape(1, k).astype(F32))
    if b_layer is None:
        in_specs.append(pl.BlockSpec((k, tn), lambda i, j: (0, j)))
    else:
        in_specs.append(pl.BlockSpec((None, k, tn), lambda i, j: (b_layer, 0, j)))
    args.append(b)
    for arr, blk, imap in extras:
        in_specs.append(pl.BlockSpec(blk, imap))
        args.append(arr)
    res = pl.pallas_call(
        functools.partial(_matmul_body, n_extra=len(extras), n_out=len(outs), norm=norm, epilogue=epilogue),
        grid=(m // tm, n // tn),
        in_specs=in_specs,
        out_specs=[pl.BlockSpec(blk, imap) for _, blk, imap in outs],
        out_shape=[sds for sds, _, _ in outs],
        scratch_shapes=[pltpu.VMEM((tm, k), BF16)] if norm else [],
        compiler_params=_cparams(("parallel", "arbitrary"), vmem_mib),
        name=name,
    )(*args)
    return res


def _rope_table_body(pos_ref, inv_ref, t_ref):
    ang = pos_ref[...].astype(F32) * inv_ref[...]
    c = jnp.cos(ang)
    s = jnp.sin(ang)
    lane = lax.broadcasted_iota(I32, ang.shape, 1)
    t_ref[...] = jnp.where(lane < 2 * (QK_ROPE // 2), c, jnp.where(lane < 3 * (QK_ROPE // 2), -s, s))


def _rope_table(positions_col, tm=512):
    s = positions_col.shape[0]
    tm = min(tm, s)
    half = QK_ROPE // 2
    inv = ROPE_THETA ** (-jnp.arange(half, dtype=F32) / half)
    inv4 = jnp.tile(inv, 4).reshape(1, LANES)
    return pl.pallas_call(
        _rope_table_body,
        grid=(s // tm,),
        in_specs=[pl.BlockSpec((tm, 1), lambda i: (i, 0)), pl.BlockSpec((1, LANES), lambda i: (0, 0))],
        out_specs=pl.BlockSpec((tm, LANES), lambda i: (i, 0)),
        out_shape=jax.ShapeDtypeStruct((s, LANES), F32),
        compiler_params=_cparams(("parallel",), 16),
        name="rope_table",
    )(positions_col, inv4)


def _rope_128(v, table):
    t = v * table
    return t + pltpu.roll(t, 2 * (QK_ROPE // 2), axis=1)


def _softmax_step_t(st, vt, m_ref, acc_ref):
    m_prev = m_ref[...]
    m_new = jnp.maximum(m_prev, jnp.max(st, axis=0, keepdims=True))
    alpha = jnp.exp2(m_prev - m_new)
    p = jnp.exp2(st - m_new)
    acc_ref[...] = acc_ref[...] * alpha + jnp.dot(vt, p.astype(BF16), preferred_element_type=F32)
    m_ref[...] = m_new


def _causal_bias(tk, tq):
    key = jnp.arange(tq, dtype=I32).reshape(tq // tk, tk, 1)
    qry = jnp.arange(tq, dtype=I32).reshape(1, 1, tq)
    return jnp.where(key <= qry, 0.0, NEG).astype(F32)


def _init_stats(m_ref, acc_ref):
    m_ref[...] = jnp.full(m_ref.shape, NEG, F32)
    acc_ref[...] = jnp.zeros(acc_ref.shape, F32)


def _normalized(acc_ref, chain, dv):
    return acc_ref[chain, :dv, :] / acc_ref[chain, dv:dv + 1, :]


def _causal_pipeline(n_common, n_diag, chains, scores, update, s_ref, unroll=ATT_UNROLL):
    assert unroll % 2 == 0

    def ahead(c, slot):
        for a, st in zip(chains, scores(c)):
            s_ref[2 * a + slot] = st

    def fold(c, slot, d=None):
        for a in chains:
            update(c, a, s_ref[2 * a + slot], d)

    def step(c, t):
        ahead(c + 1, (t + 1) % 2)
        fold(c, t % 2)

    ahead(0, 0)
    n_groups = n_common // unroll

    def body(g, carry):
        for t in range(unroll):
            step(g * unroll + t, t)
        return carry

    lax.fori_loop(0, n_groups, body, 0)
    c0 = n_groups * unroll
    rem = n_common - c0
    for t in range(unroll - 1):
        @pl.when(t < rem)
        def _(t=t):
            step(c0 + t, t)

    for parity in range(2):
        @pl.when(rem % 2 == parity)
        def _(parity=parity):
            for d in range(n_diag):
                if d + 1 < n_diag:
                    ahead(n_common + d + 1, (parity + d + 1) % 2)
                fold(n_common + d, (parity + d) % 2, d)


def _mla_body(qt_ref, k_ref, vt_ref, mask_ref, o_ref, m_ref, acc_ref, s_ref, *, tq, tk):
    i = pl.program_id(1)
    _init_stats(m_ref, acc_ref)
    qt = qt_ref[...]

    def scores(c):
        return [jnp.dot(k_ref[pl.ds(pl.multiple_of(c * tk, tk), tk), :], qt, preferred_element_type=F32)]

    def update(c, a, st, d):
        if d is not None:
            st = st + mask_ref[d]
        _softmax_step_t(st, vt_ref[c], m_ref.at[a], acc_ref.at[a])

    _causal_pipeline(i * (tq // tk), tq // tk, [0], scores, update, s_ref)
    o_ref[...] = _normalized(acc_ref, 0, o_ref.shape[1]).T.astype(o_ref.dtype)


def _mla_attention(qt, k, vt):
    h, s, dk = k.shape
    _, nchunk, dvp, tk = vt.shape
    dv = dvp - ONES_ROWS
    tq = max(min(ATT_TQ, s), tk)
    return pl.pallas_call(
        functools.partial(_mla_body, tq=tq, tk=tk),
        grid=(h, s // tq),
        in_specs=[
            pl.BlockSpec((None, dk, tq), lambda hh, i: (hh, 0, i)),
            pl.BlockSpec((None, s, dk), lambda hh, i: (hh, 0, 0)),
            pl.BlockSpec((None, nchunk, dvp, tk), lambda hh, i: (hh, 0, 0, 0)),
            pl.BlockSpec((tq // tk, tk, tq), lambda hh, i: (0, 0, 0), pipeline_mode=pl.Buffered(1)),
        ],
        out_specs=pl.BlockSpec((tq, dv), lambda hh, i: (i, hh)),
        out_shape=jax.ShapeDtypeStruct((s, h * dv), BF16),
        scratch_shapes=[pltpu.VMEM((1, 1, tq), F32), pltpu.VMEM((1, dvp, tq), F32), pltpu.VMEM((2, tk, tq), F32)],
        compiler_params=_cparams(("parallel", "arbitrary"), 48),
        name="mla_attention",
    )(qt, k, vt, _causal_bias(tk, tq))


def _diff_body(slope_ref, qt_ref, k_ref, vt_ref, mask_ref, pcol_ref, prow_ref, lq1_ref, lk1_ref, lq2_ref, lk2_ref,
               sub_ref, o_ref, m_ref, acc_ref, s_ref, *, tq, tk, lam_init):
    h = pl.program_id(0)
    i = pl.program_id(1)
    _init_stats(m_ref, acc_ref)
    qts = [qt_ref[c * DH_B:(c + 1) * DH_B, :] for c in range(2)]
    pq = prow_ref[...]
    slope = slope_ref[h]

    def scores(c):
        start = pl.multiple_of(c * tk, tk)
        k = k_ref[pl.ds(start, tk), :]
        bias = jnp.abs(pcol_ref[pl.ds(start, tk), :] - pq) * slope
        return [jnp.dot(k[:, mp * DH_B:(mp + 1) * DH_B], qts[mp], preferred_element_type=F32) - bias
                for mp in range(2)]

    def update(c, chain, st, d):
        if d is not None:
            st = st + mask_ref[d]
        _softmax_step_t(st, vt_ref[c], m_ref.at[chain], acc_ref.at[chain])

    _causal_pipeline(i * (tq // tk), tq // tk, [0, 1], scores, update, s_ref)

    lam = (jnp.exp(jnp.sum(lq1_ref[...] * lk1_ref[...], axis=1, keepdims=True))
           - jnp.exp(jnp.sum(lq2_ref[...] * lk2_ref[...], axis=1, keepdims=True)) + lam_init)
    dv = o_ref.shape[1]
    o = _normalized(acc_ref, 0, dv) - lam * _normalized(acc_ref, 1, dv)
    ms = jnp.mean(o * o, axis=0, keepdims=True)
    y = o * lax.rsqrt(ms + EPS) * sub_ref[...] * (1.0 - lam_init)
    o_ref[...] = y.T.astype(o_ref.dtype)


def _diff_attention(qt, k, vt, pos_col, pos_row, slopes_l2, lam_vecs, subln, lam_init):
    _, s, dqk = k.shape
    _, nchunk, dvp, tk = vt.shape
    dv = dvp - ONES_ROWS
    tq = max(min(ATT_TQ, s), tk)
    vec = pl.BlockSpec((1, DH_B), lambda hh, i: (0, 0))
    once = pl.Buffered(1)
    return pl.pallas_call(
        functools.partial(_diff_body, tq=tq, tk=tk, lam_init=lam_init),
        grid=(HB, s // tq),
        in_specs=[
            pl.BlockSpec(memory_space=pltpu.SMEM),
            pl.BlockSpec((None, dqk, tq), lambda hh, i: (hh, 0, i)),
            pl.BlockSpec((None, s, dqk), lambda hh, i: (hh, 0, 0), pipeline_mode=once),
            pl.BlockSpec((None, nchunk, dvp, tk), lambda hh, i: (hh, 0, 0, 0), pipeline_mode=once),
            pl.BlockSpec((tq // tk, tk, tq), lambda hh, i: (0, 0, 0), pipeline_mode=once),
            pl.BlockSpec((s, 1), lambda hh, i: (0, 0), pipeline_mode=once),
            pl.BlockSpec((1, tq), lambda hh, i: (0, i)),
            vec, vec, vec, vec,
            pl.BlockSpec((dv, 1), lambda hh, i: (0, 0)),
        ],
        out_specs=pl.BlockSpec((tq, dv), lambda hh, i: (i, hh)),
        out_shape=jax.ShapeDtypeStruct((s, HB * dv), BF16),
        scratch_shapes=[pltpu.VMEM((2, 1, tq), F32), pltpu.VMEM((2, dvp, tq), F32), pltpu.VMEM((4, tk, tq), F32)],
        compiler_params=_cparams(("parallel", "arbitrary"), 56),
        name="diff_attention",
    )(slopes_l2, qt, k, vt, _causal_bias(tk, tq), pos_col, pos_row, *lam_vecs, subln.reshape(dv, 1))


def _merge_body(oa_ref, ob_ref, woa_ref, wob_ref, ga_ref, gb_ref, o_ref):
    ya = jnp.dot(oa_ref[...], woa_ref[...], preferred_element_type=F32)
    yb = jnp.dot(ob_ref[...], wob_ref[...], preferred_element_type=F32)
    out = jax.nn.sigmoid(ga_ref[...].astype(F32)) * ya + jax.nn.sigmoid(gb_ref[...].astype(F32)) * yb
    o_ref[...] = out.astype(o_ref.dtype)


def _merge(o_a, o_b, w_oa, w_ob, gates, tm=1024, tn=512):
    s, ka = o_a.shape
    kb = o_b.shape[1]
    d = w_oa.shape[1]
    tm = min(tm, s)
    nj = d // tn
    return pl.pallas_call(
        _merge_body,
        grid=(s // tm, nj),
        in_specs=[
            pl.BlockSpec((tm, ka), lambda i, j: (i, 0)),
            pl.BlockSpec((tm, kb), lambda i, j: (i, 0)),
            pl.BlockSpec((ka, tn), lambda i, j: (0, j)),
            pl.BlockSpec((kb, tn), lambda i, j: (0, j)),
            pl.BlockSpec((tm, tn), lambda i, j: (i, j)),
            pl.BlockSpec((tm, tn), lambda i, j: (i, nj + j)),
        ],
        out_specs=pl.BlockSpec((tm, tn), lambda i, j: (i, j)),
        out_shape=jax.ShapeDtypeStruct((s, d), BF16),
        compiler_params=_cparams(("parallel", "arbitrary"), 48),
        name="gated_merge",
    )(o_a, o_b, w_oa, w_ob, gates, gates)


def _pack_pairs(x):
    half = x.shape[1] // 2
    lo = pltpu.bitcast(x[:, :half].astype(BF16).astype(F32), U32)
    hi = pltpu.bitcast(x[:, half:].astype(BF16).astype(F32), U32)
    return (lo >> 16) | (hi & jnp.uint32(0xFFFF0000))


def _unpack_pairs(p):
    lo = pltpu.bitcast(p << 16, F32)
    hi = pltpu.bitcast(p & jnp.uint32(0xFFFF0000), F32)
    return jnp.concatenate([lo, hi], axis=1)


def _router_body(x_ref, g_ref, whi_ref, wlo_ref, b_ref, tri_ref, ids_ref, gates_ref, hp_ref, cnt_ref):
    @pl.when(pl.program_id(0) == 0)
    def _():
        cnt_ref[...] = jnp.zeros(cnt_ref.shape, F32)

    h = _rms(x_ref[...], g_ref[...])
    hp_ref[...] = _pack_pairs(h)
    h_hi = h.astype(BF16)
    h_lo = (h - h_hi.astype(F32)).astype(BF16)
    logits = (jnp.dot(h_hi, whi_ref[...], preferred_element_type=F32)
              + jnp.dot(h_lo, whi_ref[...], preferred_element_type=F32)
              + jnp.dot(h_hi, wlo_ref[...], preferred_element_type=F32)) + b_ref[...]
    lane = lax.broadcasted_iota(I32, logits.shape, 1).astype(F32)

    def first_argmax(vals, mx):
        return jnp.min(jnp.where(vals == mx, lane, 4.0 * LANES), axis=1, keepdims=True)

    is_g = jnp.logical_and(lane >= N_EXPERTS, lane < N_EXPERTS + N_GROUPS)
    lg = jnp.where(is_g, logits, NEG)
    mg = jnp.max(lg, axis=1, keepdims=True)
    p_g = 1.0 / jnp.sum(jnp.exp(lg - mg), axis=1, keepdims=True)
    g_idx = first_argmax(lg, mg) - N_EXPERTS
    lo = g_idx * EXP_PER_GROUP
    in_grp = jnp.logical_and(lane >= lo, lane < lo + EXP_PER_GROUP)
    le = jnp.where(in_grp, logits, NEG)
    m1 = jnp.max(le, axis=1, keepdims=True)
    i1 = first_argmax(le, m1)
    den = jnp.sum(jnp.exp(le - m1), axis=1, keepdims=True)
    le2 = jnp.where(lane == i1, NEG, le)
    m2 = jnp.max(le2, axis=1, keepdims=True)
    i2 = first_argmax(le2, m2)
    p1 = 1.0 / den
    p2 = jnp.exp(m2 - m1) / den
    psum = p1 + p2
    gate1 = p_g * p1 / psum
    gate2 = p_g * p2 / psum

    hot1 = jnp.where(lane == i1, 1.0, 0.0)
    hot2 = jnp.where(lane == i2, 1.0, 0.0)
    tri = tri_ref[...]
    before1 = jnp.dot(tri, hot1.astype(BF16), preferred_element_type=F32) + cnt_ref[...]
    tot1 = jnp.sum(hot1, axis=0, keepdims=True)
    before2 = jnp.dot(tri, hot2.astype(BF16), preferred_element_type=F32) + (cnt_ref[...] + tot1)
    r1 = jnp.sum(hot1 * before1, axis=1, keepdims=True)
    r2 = jnp.sum(hot2 * before2, axis=1, keepdims=True)
    cnt_ref[...] = cnt_ref[...] + tot1 + jnp.sum(hot2, axis=0, keepdims=True)

    ids = jnp.where(lane == 0, i1, jnp.where(lane == 1, i2, jnp.where(lane == 2, r1, jnp.where(lane == 3, r2, 0.0))))
    ids_ref[...] = ids.astype(I32)
    gates_ref[...] = jnp.where(lane == 0, gate1, jnp.where(lane == 1, gate2, 0.0))


def _router(x, gain, w_hi, w_lo, bias, tm=256):
    t, d = x.shape
    tm = min(tm, t)
    tri = jnp.tril(jnp.ones((tm, tm), F32), -1).astype(BF16)
    return pl.pallas_call(
        _router_body,
        grid=(t // tm,),
        in_specs=[
            pl.BlockSpec((tm, d), lambda i: (i, 0)),
            pl.BlockSpec((1, d), lambda i: (0, 0)),
            pl.BlockSpec((d, LANES), lambda i: (0, 0)),
            pl.BlockSpec((d, LANES), lambda i: (0, 0)),
            pl.BlockSpec((1, LANES), lambda i: (0, 0)),
            pl.BlockSpec((tm, tm), lambda i: (0, 0)),
        ],
        out_specs=[pl.BlockSpec((tm, LANES), lambda i: (i, 0)), pl.BlockSpec((tm, LANES), lambda i: (i, 0)),
                   pl.BlockSpec((tm, d // 2), lambda i: (i, 0)), pl.BlockSpec((1, LANES), lambda i: (0, 0))],
        out_shape=[jax.ShapeDtypeStruct((t, LANES), I32), jax.ShapeDtypeStruct((t, LANES), F32),
                   jax.ShapeDtypeStruct((t, d // 2), U32), jax.ShapeDtypeStruct((1, LANES), F32)],
        compiler_params=_cparams(("arbitrary",), 40),
        name="moe_router",
    )(x, gain.reshape(1, d), w_hi, w_lo, bias, tri)


def _dispatch_plan(ids, counts, t):
    rows = MOE_ROWS
    nb = t * TOP_K // rows + N_EXPERTS
    e = ids[:, :TOP_K]
    rank = ids[:, TOP_K:2 * TOP_K]
    cnt = counts[0, :N_EXPERTS].astype(I32)
    padded = (cnt + rows - 1) // rows * rows
    experts = jnp.arange(N_EXPERTS, dtype=I32)
    pad_end = jnp.sum(jnp.where(experts[None, :] <= experts[:, None], padded[None, :], 0), axis=1).astype(I32)
    pad_start = pad_end - padded
    hot = e[:, :, None] == experts
    dest = jnp.sum(jnp.where(hot, pad_start, 0), axis=-1).astype(I32) + rank
    nb_used = (pad_end[-1:] // rows).astype(I32)
    blk_row = jnp.arange(nb, dtype=I32) * rows
    blk_exp = jnp.minimum(jnp.sum(pad_end[None, :] <= blk_row[:, None], axis=1), N_EXPERTS - 1).astype(I32)
    last_blk_row = jnp.where(cnt > 0, pad_end - rows, -1).astype(I32)
    return dest, blk_exp, nb_used, last_blk_row


def _dispatch_body(last_ref, nbu_ref, dest_ref, hp_ref, xs_hbm, zbuf, sem_z, sem, *, tm, rows, nb):
    i = pl.program_id(0)

    @pl.when(i == 0)
    def _():
        zbuf[...] = jnp.zeros(zbuf.shape, U32)

        def expert_fill(e):
            return pltpu.make_async_copy(zbuf, xs_hbm.at[pl.ds(pl.multiple_of(last_ref[e], rows), rows)], sem_z)

        def tail_fill(b):
            return pltpu.make_async_copy(zbuf, xs_hbm.at[pl.ds(pl.multiple_of(b * rows, rows), rows)], sem_z)

        for wait in (False, True):
            def experts(e, c, wait=wait):
                @pl.when(last_ref[e] >= 0)
                def _():
                    expert_fill(e).wait() if wait else expert_fill(e).start()
                return c

            def tail(b, c, wait=wait):
                @pl.when(b >= nbu_ref[0])
                def _():
                    tail_fill(b).wait() if wait else tail_fill(b).start()
                return c

            lax.fori_loop(0, N_EXPERTS, experts, 0)
            lax.fori_loop(0, nb, tail, 0)

    def row_copy(r, k):
        return pltpu.make_async_copy(hp_ref.at[pl.ds(r, 1)], xs_hbm.at[pl.ds(dest_ref[0, TOP_K * r + k], 1)], sem)

    for wait in (False, True):
        def rows_loop(r, c, wait=wait):
            for k in range(TOP_K):
                row_copy(r, k).wait() if wait else row_copy(r, k).start()
            return c

        lax.fori_loop(0, tm, rows_loop, 0)


def _dispatch(hp, dest, last_blk_row, nb_used, nb, tm=256):
    t, w = hp.shape
    tm = min(tm, t)
    rows = MOE_ROWS
    grid_spec = pltpu.PrefetchScalarGridSpec(
        num_scalar_prefetch=2,
        grid=(t // tm,),
        in_specs=[pl.BlockSpec((None, 1, TOP_K * tm), lambda i, lr, nu: (i, 0, 0), memory_space=pltpu.SMEM),
                  pl.BlockSpec((tm, w), lambda i, lr, nu: (i, 0))],
        out_specs=pl.BlockSpec(memory_space=pl.ANY),
        scratch_shapes=[pltpu.VMEM((rows, w), U32), pltpu.SemaphoreType.DMA(()), pltpu.SemaphoreType.DMA(())],
    )
    return pl.pallas_call(
        functools.partial(_dispatch_body, tm=tm, rows=rows, nb=nb),
        grid_spec=grid_spec,
        out_shape=jax.ShapeDtypeStruct((nb * rows, w), U32),
        compiler_params=_cparams(("arbitrary",), 16),
        name="moe_dispatch",
    )(last_blk_row, nb_used, dest.reshape(t // tm, 1, TOP_K * tm), hp)


def _expert_body(blk_exp_ref, nbu_ref, xs_ref, w1_ref, w3_ref, w2_ref, ys_ref):
    del blk_exp_ref
    b = pl.program_id(0)

    @pl.when(b < nbu_ref[0])
    def _():
        xn = _unpack_pairs(xs_ref[...]).astype(BF16)
        a1 = jnp.dot(xn, w1_ref[...], preferred_element_type=F32)
        a3 = jnp.dot(xn, w3_ref[...], preferred_element_type=F32)
        hmid = (jax.nn.silu(a1) * a3).astype(BF16)
        ys_ref[...] = _pack_pairs(jnp.dot(hmid, w2_ref[...], preferred_element_type=F32))

    @pl.when(b >= nbu_ref[0])
    def _():
        ys_ref[...] = jnp.zeros(ys_ref.shape, U32)


def _experts(xs, blk_exp, nb_used, w1, w3, w2):
    p, w = xs.shape
    _, d, de = w1.shape
    rows = MOE_ROWS
    grid_spec = pltpu.PrefetchScalarGridSpec(
        num_scalar_prefetch=2,
        grid=(p // rows,),
        in_specs=[
            pl.BlockSpec((rows, w), lambda b, be, nu: (jnp.minimum(b, nu[0] - 1), 0)),
            pl.BlockSpec((None, d, de), lambda b, be, nu: (be[b], 0, 0)),
            pl.BlockSpec((None, d, de), lambda b, be, nu: (be[b], 0, 0)),
            pl.BlockSpec((None, de, d), lambda b, be, nu: (be[b], 0, 0)),
        ],
        out_specs=pl.BlockSpec((rows, w), lambda b, be, nu: (b, 0)),
    )
    return pl.pallas_call(
        _expert_body,
        grid_spec=grid_spec,
        out_shape=jax.ShapeDtypeStruct((p, w), U32),
        compiler_params=_cparams(("arbitrary",), 40),
        name="moe_experts",
    )(blk_exp, nb_used, xs, w1, w3, w2)


def _combine_body(dest_ref, dest_next_ref, x_ref, gates_ref, g_ref, ys_hbm, o_ref, ybuf, sem, *, tm, final):
    i = pl.program_id(0)
    slot = lax.rem(i, 2)

    def rows(d_ref, s, wait):
        def rows_loop(r, c):
            for k in range(TOP_K):
                cp = pltpu.make_async_copy(ys_hbm.at[pl.ds(d_ref[0, TOP_K * r + k], 1)],
                                           ybuf.at[s, k, pl.ds(r, 1)], sem.at[s])
                cp.wait() if wait else cp.start()
            return c

        lax.fori_loop(0, tm, rows_loop, 0)

    @pl.when(i == 0)
    def _():
        rows(dest_ref, 0, False)

    @pl.when(i + 1 < pl.num_programs(0))
    def _():
        rows(dest_next_ref, 1 - slot, False)

    rows(dest_ref, slot, True)
    gates = gates_ref[...]
    x = x_ref[...] + (gates[:, 0:1] * _unpack_pairs(ybuf[slot, 0]) + gates[:, 1:2] * _unpack_pairs(ybuf[slot, 1]))
    o_ref[...] = _rms(x, g_ref[...]) if final else x


def _combine(x, ys, dest, gates, gain, final, tm=256):
    t, d = x.shape
    tm = min(tm, t)
    w = ys.shape[1]
    n = t // tm
    dest3 = dest.reshape(n, 1, TOP_K * tm)
    return pl.pallas_call(
        functools.partial(_combine_body, tm=tm, final=final),
        grid=(n,),
        in_specs=[pl.BlockSpec((None, 1, TOP_K * tm), lambda i: (i, 0, 0), memory_space=pltpu.SMEM),
                  pl.BlockSpec((None, 1, TOP_K * tm), lambda i: (jnp.minimum(i + 1, n - 1), 0, 0),
                               memory_space=pltpu.SMEM),
                  pl.BlockSpec((tm, d), lambda i: (i, 0)),
                  pl.BlockSpec((tm, LANES), lambda i: (i, 0)),
                  pl.BlockSpec((1, d), lambda i: (0, 0)),
                  pl.BlockSpec(memory_space=pl.ANY)],
        out_specs=pl.BlockSpec((tm, d), lambda i: (i, 0)),
        out_shape=jax.ShapeDtypeStruct((t, d), F32),
        scratch_shapes=[pltpu.VMEM((2, TOP_K, tm, w), U32), pltpu.SemaphoreType.DMA((2,))],
        compiler_params=_cparams(("arbitrary",), 48),
        name="moe_combine",
    )(dest3, dest3, x, gates, gain.reshape(1, d), ys)


def _split_w_in_body(w_ref, lat_ref, qb_ref, kb_ref, vb_ref, gate_ref):
    w = w_ref[...]
    d = gate_ref.shape[1] // 2
    o = np.cumsum([0, Q_LORA, KV_LORA, QK_ROPE, HB * 2 * DH_B, HB * 2 * DH_B, HB * 2 * DH_B, 2 * d])
    pad = jnp.zeros((w.shape[0], lat_ref.shape[1] - Q_LORA - KV_LORA - LANES), w.dtype)
    lat_ref[...] = jnp.concatenate([w[:, :o[2]], _rope_cols(w[:, o[2]:o[3]]), pad], axis=1).astype(BF16)
    qb_ref[...] = w[:, o[3]:o[4]].astype(BF16)
    kb_ref[...] = w[:, o[4]:o[5]].astype(BF16)
    vb_ref[...] = w[:, o[5]:o[6]].astype(BF16)
    gate_ref[...] = w[:, o[6]:o[7]].astype(BF16)


def _split_w_in(w_in, tr=128):
    layers, d, d_in = w_in.shape
    wb = HB * 2 * DH_B
    widths = (2048, wb, wb, wb, 2 * d)
    return pl.pallas_call(
        _split_w_in_body,
        grid=(layers, d // tr),
        in_specs=[pl.BlockSpec((None, tr, d_in), lambda l, i: (l, i, 0))],
        out_specs=[pl.BlockSpec((None, tr, wd), lambda l, i: (l, i, 0)) for wd in widths],
        out_shape=[jax.ShapeDtypeStruct((layers, d, wd), BF16) for wd in widths],
        compiler_params=_cparams(("parallel", "parallel"), 48),
        name="split_w_in",
    )(w_in)


def _rope_cols(w_rope):
    half = QK_ROPE // 2
    x1, x2 = w_rope[..., :half], w_rope[..., half:]
    return jnp.concatenate([x1, x2, x2, x1], axis=-1)


def _layer_weights(w_uq, w_ukv, w_router_g, b_router_g, w_router_e, b_router_e):
    d = w_router_g.shape[0]
    wq = w_uq.reshape(Q_LORA, HA, QK_NOPE + QK_ROPE)
    wq = jnp.concatenate([wq[..., :QK_NOPE], _rope_cols(wq[..., QK_NOPE:])], axis=-1).reshape(Q_LORA, HA * 2 * LANES)
    w_r = jnp.concatenate([w_router_e, w_router_g, jnp.zeros((d, LANES - N_EXPERTS - N_GROUPS), F32)], axis=1)
    w_r_hi = w_r.astype(BF16)
    w_r_lo = (w_r - w_r_hi.astype(F32)).astype(BF16)
    b_r = jnp.concatenate([b_router_e, b_router_g, jnp.zeros((LANES - N_EXPERTS - N_GROUPS,), F32)]).reshape(1, LANES)
    return wq.astype(BF16), w_ukv.astype(BF16), w_r_hi, w_r_lo, b_r


def kernel(x, positions, norm_attn, w_in, q_norm, w_uq, kv_norm, w_ukv, lam_q1, lam_k1, lam_q2, lam_k2, subln, w_oa,
           w_ob, w_out, norm_ffn, w_router_g, b_router_g, w_router_e, b_router_e, w1, w3, w2, norm_final):
    bsz, s, d = x.shape
    assert bsz == 1
    depth = w_in.shape[0]
    x = x.reshape(s, d)
    pos_col = positions.reshape(s, 1)
    table = _rope_table(pos_col)
    pos_col_f = pos_col.astype(F32)
    pos_row_f = pos_col_f.reshape(1, s)
    slopes_l2 = jnp.asarray(2.0 ** (-8.0 * (np.arange(HB) + 1) / HB) * LOG2E, dtype=F32)
    c_a = (QK_NOPE + QK_ROPE) ** -0.5 * LOG2E
    c_b = DH_B ** -0.5 * LOG2E
    tm = min(1024, s)
    tk_a, tk_b = min(ATT_TK_A, s), min(ATT_TK_B, s)
    hd = 2 * LANES

    def chunked_transpose(acc, tk):
        t = jnp.concatenate([acc.T, jnp.ones((ONES_ROWS, tm), F32)], axis=0)
        return jnp.stack([t[:, c * tk:(c + 1) * tk] for c in range(tm // tk)])

    w_lat, w_qb, w_kb, w_vb, w_gate = _split_w_in(w_in)
    n_exp = w1.shape[1]
    w1b, w3b, w2b = (w.reshape((depth * n_exp,) + w.shape[2:]).astype(BF16) for w in (w1, w3, w2))

    for l in range(depth):
        wq, wkv, w_r_hi, w_r_lo, b_r = _layer_weights(
            w_uq[l], w_ukv[l], w_router_g[l], b_router_g[l], w_router_e[l], b_router_e[l])
        h = _rmsnorm(x, norm_attn[l], BF16)
        full = ((tm, d), lambda i, j: (i, 0))

        def ident(acc):
            return (acc,)

        (lat,) = _matmul("in_proj_latent", h, *full, w_lat, 512,
                         [(jax.ShapeDtypeStruct((s, 2048), BF16), (tm, 512), lambda i, j: (i, j))], ident, b_layer=l)

        (qt_b,) = _matmul("in_proj_q_b", h, *full, w_qb, hd,
                          [(jax.ShapeDtypeStruct((HB, hd, s), BF16), (None, hd, tm), lambda i, j: (j, 0, i))],
                          lambda acc: ((acc * c_b).T,), b_layer=l)
        (k_b,) = _matmul("in_proj_k_b", h, *full, w_kb, hd,
                         [(jax.ShapeDtypeStruct((HB, s, hd), BF16), (None, tm, hd), lambda i, j: (j, i, 0))], ident,
                         b_layer=l)
        (vt_b,) = _matmul("in_proj_v_b", h, *full, w_vb, hd,
                          [(jax.ShapeDtypeStruct((HB, s // tk_b, hd + ONES_ROWS, tk_b), BF16),
                            (None, tm // tk_b, hd + ONES_ROWS, tk_b),
                            lambda i, j: (j, i, 0, 0))], lambda acc: (chunked_transpose(acc, tk_b),), b_layer=l)
        (gates,) = _matmul("in_proj_gates", h, *full, w_gate, 512,
                           [(jax.ShapeDtypeStruct((s, 2 * d), BF16), (tm, 512), lambda i, j: (i, j))], ident,
                           b_layer=l)

        def q_epilogue(acc, t_ref):
            rot = _rope_128(acc[:, LANES:], t_ref[...])
            return ((jnp.concatenate([acc[:, :LANES], rot], axis=1) * c_a).T,)

        (qt_a,) = _matmul("q_up_proj", lat, (tm, Q_LORA), lambda i, j: (i, 0), wq, hd,
                          [(jax.ShapeDtypeStruct((HA, hd, s), BF16), (None, hd, tm), lambda i, j: (j, 0, i))],
                          q_epilogue, gain=q_norm[l], extras=[(table, (tm, LANES), lambda i, j: (i, 0))])

        def kv_epilogue(acc, kr_ref, t_ref):
            rot = _rope_128(kr_ref[...].astype(F32), t_ref[...])
            lane = lax.broadcasted_iota(I32, rot.shape, 1)
            rot = jnp.where(lane < QK_ROPE, rot, 0.0)
            return jnp.concatenate([acc[:, :LANES], rot], axis=1), chunked_transpose(acc[:, LANES:], tk_a)

        k_a, vt_a = _matmul("kv_up_proj", lat, (tm, KV_LORA), lambda i, j: (i, Q_LORA // KV_LORA), wkv, hd,
                            [(jax.ShapeDtypeStruct((HA, s, hd), BF16), (None, tm, hd), lambda i, j: (j, i, 0)),
                             (jax.ShapeDtypeStruct((HA, s // tk_a, V_DIM_A + ONES_ROWS, tk_a), BF16),
                              (None, tm // tk_a, V_DIM_A + ONES_ROWS, tk_a), lambda i, j: (j, i, 0, 0))],
                            kv_epilogue, gain=kv_norm[l],
                            extras=[(lat, (tm, LANES), lambda i, j: (i, (Q_LORA + KV_LORA) // LANES)),
                                    (table, (tm, LANES), lambda i, j: (i, 0))])
        o_a = _mla_attention(qt_a, k_a, vt_a)

        lam_init = 0.8 - 0.6 * math.exp(-0.3 * l)
        lam_vecs = [v[l].reshape(1, DH_B).astype(F32) for v in (lam_q1, lam_k1, lam_q2, lam_k2)]
        o_b = _diff_attention(qt_b, k_b, vt_b, pos_col_f, pos_row_f, slopes_l2, lam_vecs, subln[l], lam_init)

        merged = _merge(o_a, o_b, w_oa[l].astype(BF16), w_ob[l].astype(BF16), gates)

        def residual(acc, x_ref):
            return (x_ref[...] + acc,)

        (x,) = _matmul("out_proj", merged, (tm, d), lambda i, j: (i, 0), w_out[l].astype(BF16), 512,
                       [(jax.ShapeDtypeStruct((s, d), F32), (tm, 512), lambda i, j: (i, j))], residual,
                       extras=[(x, (tm, 512), lambda i, j: (i, j))])

        ids, gate_vals, hp, counts = _router(x, norm_ffn[l], w_r_hi, w_r_lo, b_r)
        dest, blk_exp, nb_used, last_blk_row = _dispatch_plan(ids, counts, s)
        xs = _dispatch(hp, dest, last_blk_row, nb_used, blk_exp.shape[0])
        ys = _experts(xs, blk_exp + l * n_exp, nb_used, w1b, w3b, w2b)
        final = l == depth - 1
        x = _combine(x, ys, dest, gate_vals, norm_final if final else norm_ffn[l], final)
    return x.reshape(bsz, s, d)
```

```python
import functools
import math

import jax
import jax.numpy as jnp
import numpy as np
from jax import lax
from jax.experimental import pallas as pl
from jax.experimental.pallas import tpu as pltpu

F32 = jnp.float32
BF16 = jnp.bfloat16
I32 = jnp.int32
U32 = jnp.uint32

EPS = 1e-6
ROPE_THETA = 10000.0
HA, QK_NOPE, QK_ROPE, V_DIM_A = 16, 128, 64, 128
Q_LORA, KV_LORA = 1024, 512
HB, DH_B = 8, 128
N_GROUPS, EXP_PER_GROUP, TOP_K = 8, 8, 2
N_EXPERTS = N_GROUPS * EXP_PER_GROUP
LOG2E = 1.4426950408889634
NEG = -1e30

LANES = 128
MOE_ROWS = 128
ATT_TK_A = 1024
ATT_TK_B = 512
ATT_TQ = 1024
ATT_UNROLL = 2
ROW_DMA_UNROLL = 8
ONES_ROWS = 16
MIB = 1024 * 1024


def _cparams(semantics, vmem_mib):
    return pltpu.CompilerParams(dimension_semantics=semantics, vmem_limit_bytes=vmem_mib * MIB)


def _rms(xf, gain):
    ms = jnp.mean(xf * xf, axis=-1, keepdims=True)
    return xf * lax.rsqrt(ms + EPS) * gain


def _rmsnorm_body(x_ref, g_ref, o_ref):
    o_ref[...] = _rms(x_ref[...], g_ref[...]).astype(o_ref.dtype)


def _rmsnorm(x, gain, out_dtype, tm=256):
    s, d = x.shape
    tm = min(tm, s)
    return pl.pallas_call(
        _rmsnorm_body,
        grid=(s // tm,),
        in_specs=[pl.BlockSpec((tm, d), lambda i: (i, 0)), pl.BlockSpec((1, d), lambda i: (0, 0))],
        out_specs=pl.BlockSpec((tm, d), lambda i: (i, 0)),
        out_shape=jax.ShapeDtypeStruct((s, d), out_dtype),
        compiler_params=_cparams(("parallel",), 32),
        name="rmsnorm",
    )(x, gain.reshape(1, d))


def _matmul_body(*refs, n_extra, n_out, norm, epilogue):
    a_ref = refs[0]
    pos = 1
    if norm:
        g_ref = refs[pos]
        pos += 1
    b_ref = refs[pos]
    pos += 1
    extras = refs[pos:pos + n_extra]
    pos += n_extra
    outs = refs[pos:pos + n_out]
    pos += n_out
    if norm:
        lhs_ref = refs[pos]

        @pl.when(pl.program_id(1) == 0)
        def _():
            lhs_ref[...] = _rms(a_ref[...].astype(F32), g_ref[...]).astype(BF16)

        lhs = lhs_ref[...]
    else:
        lhs = a_ref[...]
    acc = jnp.dot(lhs, b_ref[...], preferred_element_type=F32)
    for o_ref, r in zip(outs, epilogue(acc, *extras)):
        o_ref[...] = r.astype(o_ref.dtype)


def _matmul(name, a, a_block, a_index, b, tn, outs, epilogue, gain=None, extras=(), vmem_mib=48, b_layer=None):
    tm, k = a_block
    m = a.shape[0]
    n = b.shape[-1]
    norm = gain is not None
    in_specs = [pl.BlockSpec(a_block, a_index)]
    args = [a]
    if norm:
        in_specs.append(pl.BlockSpec((1, k), lambda i, j: (0, 0)))
        args.append(gain.reshape(1, k).astype(F32))
    if b_layer is None:
        in_specs.append(pl.BlockSpec((k, tn), lambda i, j: (0, j)))
    else:
        in_specs.append(pl.BlockSpec((None, k, tn), lambda i, j: (b_layer, 0, j)))
    args.append(b)
    for arr, blk, imap in extras:
        in_specs.append(pl.BlockSpec(blk, imap))
        args.append(arr)
    res = pl.pallas_call(
        functools.partial(_matmul_body, n_extra=len(extras), n_out=len(outs), norm=norm, epilogue=epilogue),
        grid=(m // tm, n // tn),
        in_specs=in_specs,
        out_specs=[pl.BlockSpec(blk, imap) for _, blk, imap in outs],
        out_shape=[sds for sds, _, _ in outs],
        scratch_shapes=[pltpu.VMEM((tm, k), BF16)] if norm else [],
        compiler_params=_cparams(("parallel", "arbitrary"), vmem_mib),
        name=name,
    )(*args)
    return res


def _rope_table_body(pos_ref, inv_ref, t_ref):
    ang = pos_ref[...].astype(F32) * inv_ref[...]
    c = jnp.cos(ang)
    s = jnp.sin(ang)
    lane = lax.broadcasted_iota(I32, ang.shape, 1)
    t_ref[...] = jnp.where(lane < 2 * (QK_ROPE // 2), c, jnp.where(lane < 3 * (QK_ROPE // 2), -s, s))


def _rope_table(positions_col, tm=512):
    s = positions_col.shape[0]
    tm = min(tm, s)
    half = QK_ROPE // 2
    inv = ROPE_THETA ** (-jnp.arange(half, dtype=F32) / half)
    inv4 = jnp.tile(inv, 4).reshape(1, LANES)
    return pl.pallas_call(
        _rope_table_body,
        grid=(s // tm,),
        in_specs=[pl.BlockSpec((tm, 1), lambda i: (i, 0)), pl.BlockSpec((1, LANES), lambda i: (0, 0))],
        out_specs=pl.BlockSpec((tm, LANES), lambda i: (i, 0)),
        out_shape=jax.ShapeDtypeStruct((s, LANES), F32),
        compiler_params=_cparams(("parallel",), 16),
        name="rope_table",
    )(positions_col, inv4)


def _rope_128(v, table):
    t = v * table
    return t + pltpu.roll(t, 2 * (QK_ROPE // 2), axis=1)


def _softmax_step_t(st, vt, m_ref, acc_ref):
    m_prev = m_ref[...]
    m_new = jnp.maximum(m_prev, jnp.max(st, axis=0, keepdims=True))
    alpha = jnp.exp2(m_prev - m_new)
    p = jnp.exp2(st - m_new)
    acc_ref[...] = acc_ref[...] * alpha + jnp.dot(vt, p.astype(BF16), preferred_element_type=F32)
    m_ref[...] = m_new


def _causal_bias(tk, tq):
    key = jnp.arange(tq, dtype=I32).reshape(tq // tk, tk, 1)
    qry = jnp.arange(tq, dtype=I32).reshape(1, 1, tq)
    return jnp.where(key <= qry, 0.0, NEG).astype(F32)


def _init_stats(m_ref, acc_ref):
    m_ref[...] = jnp.full(m_ref.shape, NEG, F32)
    acc_ref[...] = jnp.zeros(acc_ref.shape, F32)


def _normalized(acc_ref, chain, dv):
    return acc_ref[chain, :dv, :] / acc_ref[chain, dv:dv + 1, :]


def _causal_pipeline(n_common, n_diag, chains, scores, update, s_ref, unroll=ATT_UNROLL):
    assert unroll % 2 == 0

    def ahead(c, slot):
        for a, st in zip(chains, scores(c)):
            s_ref[2 * a + slot] = st

    def fold(c, slot, d=None):
        for a in chains:
            update(c, a, s_ref[2 * a + slot], d)

    def step(c, t):
        ahead(c + 1, (t + 1) % 2)
        fold(c, t % 2)

    ahead(0, 0)
    n_groups = n_common // unroll

    def body(g, carry):
        for t in range(unroll):
            step(g * unroll + t, t)
        return carry

    lax.fori_loop(0, n_groups, body, 0)
    c0 = n_groups * unroll
    rem = n_common - c0
    for t in range(unroll - 1):
        @pl.when(t < rem)
        def _(t=t):
            step(c0 + t, t)

    for parity in range(2):
        @pl.when(rem % 2 == parity)
        def _(parity=parity):
            for d in range(n_diag):
                if d + 1 < n_diag:
                    ahead(n_common + d + 1, (parity + d + 1) % 2)
                fold(n_common + d, (parity + d) % 2, d)


def _mla_body(qt_ref, k_ref, vt_ref, mask_ref, o_ref, m_ref, acc_ref, s_ref, *, tq, tk):
    i = pl.program_id(1)
    _init_stats(m_ref, acc_ref)
    qt = qt_ref[...]

    def scores(c):
        return [jnp.dot(k_ref[pl.ds(pl.multiple_of(c * tk, tk), tk), :], qt, preferred_element_type=F32)]

    def update(c, a, st, d):
        if d is not None:
            st = st + mask_ref[d]
        _softmax_step_t(st, vt_ref[c], m_ref.at[a], acc_ref.at[a])

    _causal_pipeline(i * (tq // tk), tq // tk, [0], scores, update, s_ref)
    o_ref[...] = _normalized(acc_ref, 0, o_ref.shape[1]).T.astype(o_ref.dtype)


def _mla_attention(qt, k, vt):
    h, s, dk = k.shape
    _, nchunk, dvp, tk = vt.shape
    dv = dvp - ONES_ROWS
    tq = max(min(ATT_TQ, s), tk)
    return pl.pallas_call(
        functools.partial(_mla_body, tq=tq, tk=tk),
        grid=(h, s // tq),
        in_specs=[
            pl.BlockSpec((None, dk, tq), lambda hh, i: (hh, 0, i)),
            pl.BlockSpec((None, s, dk), lambda hh, i: (hh, 0, 0)),
            pl.BlockSpec((None, nchunk, dvp, tk), lambda hh, i: (hh, 0, 0, 0)),
            pl.BlockSpec((tq // tk, tk, tq), lambda hh, i: (0, 0, 0), pipeline_mode=pl.Buffered(1)),
        ],
        out_specs=pl.BlockSpec((tq, dv), lambda hh, i: (i, hh)),
        out_shape=jax.ShapeDtypeStruct((s, h * dv), BF16),
        scratch_shapes=[pltpu.VMEM((1, 1, tq), F32), pltpu.VMEM((1, dvp, tq), F32), pltpu.VMEM((2, tk, tq), F32)],
        compiler_params=_cparams(("parallel", "arbitrary"), 48),
        name="mla_attention",
    )(qt, k, vt, _causal_bias(tk, tq))


def _diff_body(slope_ref, qt_ref, k_ref, vt_ref, mask_ref, pcol_ref, prow_ref, lq1_ref, lk1_ref, lq2_ref, lk2_ref,
               sub_ref, o_ref, m_ref, acc_ref, s_ref, *, tq, tk, lam_init):
    h = pl.program_id(0)
    i = pl.program_id(1)
    _init_stats(m_ref, acc_ref)
    qts = [qt_ref[c * DH_B:(c + 1) * DH_B, :] for c in range(2)]
    pq = prow_ref[...]
    slope = slope_ref[h]

    def scores(c):
        start = pl.multiple_of(c * tk, tk)
        k = k_ref[pl.ds(start, tk), :]
        bias = jnp.abs(pcol_ref[pl.ds(start, tk), :] - pq) * slope
        return [jnp.dot(k[:, mp * DH_B:(mp + 1) * DH_B], qts[mp], preferred_element_type=F32) - bias
                for mp in range(2)]

    def update(c, chain, st, d):
        if d is not None:
            st = st + mask_ref[d]
        _softmax_step_t(st, vt_ref[c], m_ref.at[chain], acc_ref.at[chain])

    _causal_pipeline(i * (tq // tk), tq // tk, [0, 1], scores, update, s_ref)

    lam = (jnp.exp(jnp.sum(lq1_ref[...] * lk1_ref[...], axis=1, keepdims=True))
           - jnp.exp(jnp.sum(lq2_ref[...] * lk2_ref[...], axis=1, keepdims=True)) + lam_init)
    dv = o_ref.shape[1]
    o = _normalized(acc_ref, 0, dv) - lam * _normalized(acc_ref, 1, dv)
    ms = jnp.mean(o * o, axis=0, keepdims=True)
    y = o * lax.rsqrt(ms + EPS) * sub_ref[...] * (1.0 - lam_init)
    o_ref[...] = y.T.astype(o_ref.dtype)


def _diff_attention(qt, k, vt, pos_col, pos_row, slopes_l2, lam_vecs, subln, lam_init):
    _, s, dqk = k.shape
    _, nchunk, dvp, tk = vt.shape
    dv = dvp - ONES_ROWS
    tq = max(min(ATT_TQ, s), tk)
    vec = pl.BlockSpec((1, DH_B), lambda hh, i: (0, 0))
    once = pl.Buffered(1)
    return pl.pallas_call(
        functools.partial(_diff_body, tq=tq, tk=tk, lam_init=lam_init),
        grid=(HB, s // tq),
        in_specs=[
            pl.BlockSpec(memory_space=pltpu.SMEM),
            pl.BlockSpec((None, dqk, tq), lambda hh, i: (hh, 0, i)),
            pl.BlockSpec((None, s, dqk), lambda hh, i: (hh, 0, 0), pipeline_mode=once),
            pl.BlockSpec((None, nchunk, dvp, tk), lambda hh, i: (hh, 0, 0, 0), pipeline_mode=once),
            pl.BlockSpec((tq // tk, tk, tq), lambda hh, i: (0, 0, 0), pipeline_mode=once),
            pl.BlockSpec((s, 1), lambda hh, i: (0, 0), pipeline_mode=once),
            pl.BlockSpec((1, tq), lambda hh, i: (0, i)),
            vec, vec, vec, vec,
            pl.BlockSpec((dv, 1), lambda hh, i: (0, 0)),
        ],
        out_specs=pl.BlockSpec((tq, dv), lambda hh, i: (i, hh)),
        out_shape=jax.ShapeDtypeStruct((s, HB * dv), BF16),
        scratch_shapes=[pltpu.VMEM((2, 1, tq), F32), pltpu.VMEM((2, dvp, tq), F32), pltpu.VMEM((4, tk, tq), F32)],
        compiler_params=_cparams(("parallel", "arbitrary"), 56),
        name="diff_attention",
    )(slopes_l2, qt, k, vt, _causal_bias(tk, tq), pos_col, pos_row, *lam_vecs, subln.reshape(dv, 1))


def _merge_body(oa_ref, ob_ref, woa_ref, wob_ref, ga_ref, gb_ref, o_ref):
    ya = jnp.dot(oa_ref[...], woa_ref[...], preferred_element_type=F32)
    yb = jnp.dot(ob_ref[...], wob_ref[...], preferred_element_type=F32)
    out = jax.nn.sigmoid(ga_ref[...].astype(F32)) * ya + jax.nn.sigmoid(gb_ref[...].astype(F32)) * yb
    o_ref[...] = out.astype(o_ref.dtype)


def _merge(o_a, o_b, w_oa, w_ob, gates, tm=1024, tn=512):
    s, ka = o_a.shape
    kb = o_b.shape[1]
    d = w_oa.shape[1]
    tm = min(tm, s)
    nj = d // tn
    return pl.pallas_call(
        _merge_body,
        grid=(s // tm, nj),
        in_specs=[
            pl.BlockSpec((tm, ka), lambda i, j: (i, 0)),
            pl.BlockSpec((tm, kb), lambda i, j: (i, 0)),
            pl.BlockSpec((ka, tn), lambda i, j: (0, j)),
            pl.BlockSpec((kb, tn), lambda i, j: (0, j)),
            pl.BlockSpec((tm, tn), lambda i, j: (i, j)),
            pl.BlockSpec((tm, tn), lambda i, j: (i, nj + j)),
        ],
        out_specs=pl.BlockSpec((tm, tn), lambda i, j: (i, j)),
        out_shape=jax.ShapeDtypeStruct((s, d), BF16),
        compiler_params=_cparams(("parallel", "arbitrary"), 48),
        name="gated_merge",
    )(o_a, o_b, w_oa, w_ob, gates, gates)


def _pack_pairs(x):
    half = x.shape[1] // 2
    lo = pltpu.bitcast(x[:, :half].astype(BF16).astype(F32), U32)
    hi = pltpu.bitcast(x[:, half:].astype(BF16).astype(F32), U32)
    return (lo >> 16) | (hi & jnp.uint32(0xFFFF0000))


def _unpack_pairs(p):
    lo = pltpu.bitcast(p << 16, F32)
    hi = pltpu.bitcast(p & jnp.uint32(0xFFFF0000), F32)
    return jnp.concatenate([lo, hi], axis=1)


def _router_body(x_ref, g_ref, whi_ref, wlo_ref, b_ref, tri_ref, ids_ref, gates_ref, hp_ref, cnt_ref):
    @pl.when(pl.program_id(0) == 0)
    def _():
        cnt_ref[...] = jnp.zeros(cnt_ref.shape, F32)

    h = _rms(x_ref[...], g_ref[...])
    hp_ref[...] = _pack_pairs(h)
    h_hi = h.astype(BF16)
    h_lo = (h - h_hi.astype(F32)).astype(BF16)
    logits = (jnp.dot(h_hi, whi_ref[...], preferred_element_type=F32)
              + jnp.dot(h_lo, whi_ref[...], preferred_element_type=F32)
              + jnp.dot(h_hi, wlo_ref[...], preferred_element_type=F32)) + b_ref[...]
    lane = lax.broadcasted_iota(I32, logits.shape, 1).astype(F32)

    def first_argmax(vals, mx):
        return jnp.min(jnp.where(vals == mx, lane, 4.0 * LANES), axis=1, keepdims=True)

    is_g = jnp.logical_and(lane >= N_EXPERTS, lane < N_EXPERTS + N_GROUPS)
    lg = jnp.where(is_g, logits, NEG)
    mg = jnp.max(lg, axis=1, keepdims=True)
    p_g = 1.0 / jnp.sum(jnp.exp(lg - mg), axis=1, keepdims=True)
    g_idx = first_argmax(lg, mg) - N_EXPERTS
    lo = g_idx * EXP_PER_GROUP
    in_grp = jnp.logical_and(lane >= lo, lane < lo + EXP_PER_GROUP)
    le = jnp.where(in_grp, logits, NEG)
    m1 = jnp.max(le, axis=1, keepdims=True)
    i1 = first_argmax(le, m1)
    den = jnp.sum(jnp.exp(le - m1), axis=1, keepdims=True)
    le2 = jnp.where(lane == i1, NEG, le)
    m2 = jnp.max(le2, axis=1, keepdims=True)
    i2 = first_argmax(le2, m2)
    p1 = 1.0 / den
    p2 = jnp.exp(m2 - m1) / den
    psum = p1 + p2
    gate1 = p_g * p1 / psum
    gate2 = p_g * p2 / psum

    hot1 = jnp.where(lane == i1, 1.0, 0.0)
    hot2 = jnp.where(lane == i2, 1.0, 0.0)
    tri = tri_ref[...]
    before1 = jnp.dot(tri, hot1.astype(BF16), preferred_element_type=F32) + cnt_ref[...]
    tot1 = jnp.sum(hot1, axis=0, keepdims=True)
    before2 = jnp.dot(tri, hot2.astype(BF16), preferred_element_type=F32) + (cnt_ref[...] + tot1)
    r1 = jnp.sum(hot1 * before1, axis=1, keepdims=True)
    r2 = jnp.sum(hot2 * before2, axis=1, keepdims=True)
    cnt_ref[...] = cnt_ref[...] + tot1 + jnp.sum(hot2, axis=0, keepdims=True)

    ids = jnp.where(lane == 0, i1, jnp.where(lane == 1, i2, jnp.where(lane == 2, r1, jnp.where(lane == 3, r2, 0.0))))
    ids_ref[...] = ids.astype(I32)
    gates_ref[...] = jnp.where(lane == 0, gate1, jnp.where(lane == 1, gate2, 0.0))


def _router(x, gain, w_hi, w_lo, bias, tm=256):
    t, d = x.shape
    tm = min(tm, t)
    tri = jnp.tril(jnp.ones((tm, tm), F32), -1).astype(BF16)
    return pl.pallas_call(
        _router_body,
        grid=(t // tm,),
        in_specs=[
            pl.BlockSpec((tm, d), lambda i: (i, 0)),
            pl.BlockSpec((1, d), lambda i: (0, 0)),
            pl.BlockSpec((d, LANES), lambda i: (0, 0)),
            pl.BlockSpec((d, LANES), lambda i: (0, 0)),
            pl.BlockSpec((1, LANES), lambda i: (0, 0)),
            pl.BlockSpec((tm, tm), lambda i: (0, 0)),
        ],
        out_specs=[pl.BlockSpec((tm, LANES), lambda i: (i, 0)), pl.BlockSpec((tm, LANES), lambda i: (i, 0)),
                   pl.BlockSpec((tm, d // 2), lambda i: (i, 0)), pl.BlockSpec((1, LANES), lambda i: (0, 0))],
        out_shape=[jax.ShapeDtypeStruct((t, LANES), I32), jax.ShapeDtypeStruct((t, LANES), F32),
                   jax.ShapeDtypeStruct((t, d // 2), U32), jax.ShapeDtypeStruct((1, LANES), F32)],
        compiler_params=_cparams(("arbitrary",), 40),
        name="moe_router",
    )(x, gain.reshape(1, d), w_hi, w_lo, bias, tri)


def _dispatch_plan(ids, counts, t):
    rows = MOE_ROWS
    nb = t * TOP_K // rows + N_EXPERTS
    e = ids[:, :TOP_K]
    rank = ids[:, TOP_K:2 * TOP_K]
    cnt = counts[0, :N_EXPERTS].astype(I32)
    padded = (cnt + rows - 1) // rows * rows
    experts = jnp.arange(N_EXPERTS, dtype=I32)
    pad_end = jnp.sum(jnp.where(experts[None, :] <= experts[:, None], padded[None, :], 0), axis=1).astype(I32)
    pad_start = pad_end - padded
    hot = e[:, :, None] == experts
    dest = jnp.sum(jnp.where(hot, pad_start, 0), axis=-1).astype(I32) + rank
    nb_used = (pad_end[-1:] // rows).astype(I32)
    blk_row = jnp.arange(nb, dtype=I32) * rows
    blk_exp = jnp.minimum(jnp.sum(pad_end[None, :] <= blk_row[:, None], axis=1), N_EXPERTS - 1).astype(I32)
    last_blk_row = jnp.where(cnt > 0, pad_end - rows, -1).astype(I32)
    return dest, blk_exp, nb_used, last_blk_row


def _dispatch_body(last_ref, nbu_ref, dest_ref, hp_ref, xs_hbm, zbuf, sem_z, sem, *, tm, rows, nb):
    i = pl.program_id(0)

    @pl.when(i == 0)
    def _():
        zbuf[...] = jnp.zeros(zbuf.shape, U32)

        def expert_fill(e):
            return pltpu.make_async_copy(zbuf, xs_hbm.at[pl.ds(pl.multiple_of(last_ref[e], rows), rows)], sem_z)

        def tail_fill(b):
            return pltpu.make_async_copy(zbuf, xs_hbm.at[pl.ds(pl.multiple_of(b * rows, rows), rows)], sem_z)

        for wait in (False, True):
            def experts(e, c, wait=wait):
                @pl.when(last_ref[e] >= 0)
                def _():
                    expert_fill(e).wait() if wait else expert_fill(e).start()
                return c

            def tail(b, c, wait=wait):
                @pl.when(b >= nbu_ref[0])
                def _():
                    tail_fill(b).wait() if wait else tail_fill(b).start()
                return c

            lax.fori_loop(0, N_EXPERTS, experts, 0)
            lax.fori_loop(0, nb, tail, 0)

    def start_rows(r, c):
        for k in range(TOP_K):
            pltpu.make_async_copy(hp_ref.at[pl.ds(r, 1)], xs_hbm.at[pl.ds(dest_ref[0, TOP_K * r + k], 1)], sem).start()
        return c

    lax.fori_loop(0, tm, start_rows, 0, unroll=ROW_DMA_UNROLL)
    whole = xs_hbm.at[pl.ds(0, TOP_K * tm)]
    pltpu.make_async_copy(whole, whole, sem).wait()


def _dispatch(hp, dest, last_blk_row, nb_used, nb, tm=256):
    t, w = hp.shape
    tm = min(tm, t)
    rows = MOE_ROWS
    grid_spec = pltpu.PrefetchScalarGridSpec(
        num_scalar_prefetch=2,
        grid=(t // tm,),
        in_specs=[pl.BlockSpec((None, 1, TOP_K * tm), lambda i, lr, nu: (i, 0, 0), memory_space=pltpu.SMEM),
                  pl.BlockSpec((tm, w), lambda i, lr, nu: (i, 0))],
        out_specs=pl.BlockSpec(memory_space=pl.ANY),
        scratch_shapes=[pltpu.VMEM((rows, w), U32), pltpu.SemaphoreType.DMA(()), pltpu.SemaphoreType.DMA(())],
    )
    return pl.pallas_call(
        functools.partial(_dispatch_body, tm=tm, rows=rows, nb=nb),
        grid_spec=grid_spec,
        out_shape=jax.ShapeDtypeStruct((nb * rows, w), U32),
        compiler_params=_cparams(("arbitrary",), 16),
        name="moe_dispatch",
    )(last_blk_row, nb_used, dest.reshape(t // tm, 1, TOP_K * tm), hp)


def _expert_body(blk_exp_ref, nbu_ref, xs_ref, w1_ref, w3_ref, w2_ref, ys_ref):
    del blk_exp_ref
    b = pl.program_id(0)

    @pl.when(b < nbu_ref[0])
    def _():
        xn = _unpack_pairs(xs_ref[...]).astype(BF16)
        a1 = jnp.dot(xn, w1_ref[...], preferred_element_type=F32)
        a3 = jnp.dot(xn, w3_ref[...], preferred_element_type=F32)
        hmid = (jax.nn.silu(a1) * a3).astype(BF16)
        ys_ref[...] = _pack_pairs(jnp.dot(hmid, w2_ref[...], preferred_element_type=F32))

    @pl.when(b >= nbu_ref[0])
    def _():
        ys_ref[...] = jnp.zeros(ys_ref.shape, U32)


def _experts(xs, blk_exp, nb_used, w1, w3, w2):
    p, w = xs.shape
    _, d, de = w1.shape
    rows = MOE_ROWS
    grid_spec = pltpu.PrefetchScalarGridSpec(
        num_scalar_prefetch=2,
        grid=(p // rows,),
        in_specs=[
            pl.BlockSpec((rows, w), lambda b, be, nu: (jnp.minimum(b, nu[0] - 1), 0)),
            pl.BlockSpec((None, d, de), lambda b, be, nu: (be[b], 0, 0)),
            pl.BlockSpec((None, d, de), lambda b, be, nu: (be[b], 0, 0)),
            pl.BlockSpec((None, de, d), lambda b, be, nu: (be[b], 0, 0)),
        ],
        out_specs=pl.BlockSpec((rows, w), lambda b, be, nu: (b, 0)),
    )
    return pl.pallas_call(
        _expert_body,
        grid_spec=grid_spec,
        out_shape=jax.ShapeDtypeStruct((p, w), U32),
        compiler_params=_cparams(("arbitrary",), 40),
        name="moe_experts",
    )(blk_exp, nb_used, xs, w1, w3, w2)


def _combine_body(dest_ref, dest_next_ref, x_ref, gates_ref, g_ref, ys_hbm, o_ref, ybuf, sem, *, tm, final):
    i = pl.program_id(0)
    slot = lax.rem(i, 2)

    def start_rows(d_ref, s):
        def rows_loop(r, c):
            for k in range(TOP_K):
                pltpu.make_async_copy(ys_hbm.at[pl.ds(d_ref[0, TOP_K * r + k], 1)], ybuf.at[s, k, pl.ds(r, 1)],
                                      sem.at[s]).start()
            return c

        lax.fori_loop(0, tm, rows_loop, 0, unroll=ROW_DMA_UNROLL)

    @pl.when(i == 0)
    def _():
        start_rows(dest_ref, 0)

    @pl.when(i + 1 < pl.num_programs(0))
    def _():
        start_rows(dest_next_ref, 1 - slot)

    pltpu.make_async_copy(ybuf.at[slot], ybuf.at[slot], sem.at[slot]).wait()
    gates = gates_ref[...]
    x = x_ref[...] + (gates[:, 0:1] * _unpack_pairs(ybuf[slot, 0]) + gates[:, 1:2] * _unpack_pairs(ybuf[slot, 1]))
    o_ref[...] = _rms(x, g_ref[...]) if final else x


def _combine(x, ys, dest, gates, gain, final, tm=256):
    t, d = x.shape
    tm = min(tm, t)
    w = ys.shape[1]
    n = t // tm
    dest3 = dest.reshape(n, 1, TOP_K * tm)
    return pl.pallas_call(
        functools.partial(_combine_body, tm=tm, final=final),
        grid=(n,),
        in_specs=[pl.BlockSpec((None, 1, TOP_K * tm), lambda i: (i, 0, 0), memory_space=pltpu.SMEM),
                  pl.BlockSpec((None, 1, TOP_K * tm), lambda i: (jnp.minimum(i + 1, n - 1), 0, 0),
                               memory_space=pltpu.SMEM),
                  pl.BlockSpec((tm, d), lambda i: (i, 0)),
                  pl.BlockSpec((tm, LANES), lambda i: (i, 0)),
                  pl.BlockSpec((1, d), lambda i: (0, 0)),
                  pl.BlockSpec(memory_space=pl.ANY)],
        out_specs=pl.BlockSpec((tm, d), lambda i: (i, 0)),
        out_shape=jax.ShapeDtypeStruct((t, d), F32),
        scratch_shapes=[pltpu.VMEM((2, TOP_K, tm, w), U32), pltpu.SemaphoreType.DMA((2,))],
        compiler_params=_cparams(("arbitrary",), 48),
        name="moe_combine",
    )(dest3, dest3, x, gates, gain.reshape(1, d), ys)


def _split_w_in_body(w_ref, lat_ref, qb_ref, kb_ref, vb_ref, gate_ref):
    w = w_ref[...]
    d = gate_ref.shape[1] // 2
    o = np.cumsum([0, Q_LORA, KV_LORA, QK_ROPE, HB * 2 * DH_B, HB * 2 * DH_B, HB * 2 * DH_B, 2 * d])
    pad = jnp.zeros((w.shape[0], lat_ref.shape[1] - Q_LORA - KV_LORA - LANES), w.dtype)
    lat_ref[...] = jnp.concatenate([w[:, :o[2]], _rope_cols(w[:, o[2]:o[3]]), pad], axis=1).astype(BF16)
    qb_ref[...] = w[:, o[3]:o[4]].astype(BF16)
    kb_ref[...] = w[:, o[4]:o[5]].astype(BF16)
    vb_ref[...] = w[:, o[5]:o[6]].astype(BF16)
    gate_ref[...] = w[:, o[6]:o[7]].astype(BF16)


def _split_w_in(w_in, tr=128):
    layers, d, d_in = w_in.shape
    wb = HB * 2 * DH_B
    widths = (2048, wb, wb, wb, 2 * d)
    return pl.pallas_call(
        _split_w_in_body,
        grid=(layers, d // tr),
        in_specs=[pl.BlockSpec((None, tr, d_in), lambda l, i: (l, i, 0))],
        out_specs=[pl.BlockSpec((None, tr, wd), lambda l, i: (l, i, 0)) for wd in widths],
        out_shape=[jax.ShapeDtypeStruct((layers, d, wd), BF16) for wd in widths],
        compiler_params=_cparams(("parallel", "parallel"), 48),
        name="split_w_in",
    )(w_in)


def _rope_cols(w_rope):
    half = QK_ROPE // 2
    x1, x2 = w_rope[..., :half], w_rope[..., half:]
    return jnp.concatenate([x1, x2, x2, x1], axis=-1)


def _layer_weights(w_uq, w_ukv, w_router_g, b_router_g, w_router_e, b_router_e):
    d = w_router_g.shape[0]
    wq = w_uq.reshape(Q_LORA, HA, QK_NOPE + QK_ROPE)
    wq = jnp.concatenate([wq[..., :QK_NOPE], _rope_cols(wq[..., QK_NOPE:])], axis=-1).reshape(Q_LORA, HA * 2 * LANES)
    w_r = jnp.concatenate([w_router_e, w_router_g, jnp.zeros((d, LANES - N_EXPERTS - N_GROUPS), F32)], axis=1)
    w_r_hi = w_r.astype(BF16)
    w_r_lo = (w_r - w_r_hi.astype(F32)).astype(BF16)
    b_r = jnp.concatenate([b_router_e, b_router_g, jnp.zeros((LANES - N_EXPERTS - N_GROUPS,), F32)]).reshape(1, LANES)
    return wq.astype(BF16), w_ukv.astype(BF16), w_r_hi, w_r_lo, b_r


def kernel(x, positions, norm_attn, w_in, q_norm, w_uq, kv_norm, w_ukv, lam_q1, lam_k1, lam_q2, lam_k2, subln, w_oa,
           w_ob, w_out, norm_ffn, w_router_g, b_router_g, w_router_e, b_router_e, w1, w3, w2, norm_final):
    bsz, s, d = x.shape
    assert bsz == 1
    depth = w_in.shape[0]
    x = x.reshape(s, d)
    pos_col = positions.reshape(s, 1)
    table = _rope_table(pos_col)
    pos_col_f = pos_col.astype(F32)
    pos_row_f = pos_col_f.reshape(1, s)
    slopes_l2 = jnp.asarray(2.0 ** (-8.0 * (np.arange(HB) + 1) / HB) * LOG2E, dtype=F32)
    c_a = (QK_NOPE + QK_ROPE) ** -0.5 * LOG2E
    c_b = DH_B ** -0.5 * LOG2E
    tm = min(1024, s)
    tk_a, tk_b = min(ATT_TK_A, s), min(ATT_TK_B, s)
    hd = 2 * LANES

    def chunked_transpose(acc, tk):
        t = jnp.concatenate([acc.T, jnp.ones((ONES_ROWS, tm), F32)], axis=0)
        return jnp.stack([t[:, c * tk:(c + 1) * tk] for c in range(tm // tk)])

    w_lat, w_qb, w_kb, w_vb, w_gate = _split_w_in(w_in)
    n_exp = w1.shape[1]
    w1b, w3b, w2b = (w.reshape((depth * n_exp,) + w.shape[2:]).astype(BF16) for w in (w1, w3, w2))

    for l in range(depth):
        wq, wkv, w_r_hi, w_r_lo, b_r = _layer_weights(
            w_uq[l], w_ukv[l], w_router_g[l], b_router_g[l], w_router_e[l], b_router_e[l])
        h = _rmsnorm(x, norm_attn[l], BF16)
        full = ((tm, d), lambda i, j: (i, 0))

        def ident(acc):
            return (acc,)

        (lat,) = _matmul("in_proj_latent", h, *full, w_lat, 512,
                         [(jax.ShapeDtypeStruct((s, 2048), BF16), (tm, 512), lambda i, j: (i, j))], ident, b_layer=l)

        (qt_b,) = _matmul("in_proj_q_b", h, *full, w_qb, hd,
                          [(jax.ShapeDtypeStruct((HB, hd, s), BF16), (None, hd, tm), lambda i, j: (j, 0, i))],
                          lambda acc: ((acc * c_b).T,), b_layer=l)
        (k_b,) = _matmul("in_proj_k_b", h, *full, w_kb, hd,
                         [(jax.ShapeDtypeStruct((HB, s, hd), BF16), (None, tm, hd), lambda i, j: (j, i, 0))], ident,
                         b_layer=l)
        (vt_b,) = _matmul("in_proj_v_b", h, *full, w_vb, hd,
                          [(jax.ShapeDtypeStruct((HB, s // tk_b, hd + ONES_ROWS, tk_b), BF16),
                            (None, tm // tk_b, hd + ONES_ROWS, tk_b),
                            lambda i, j: (j, i, 0, 0))], lambda acc: (chunked_transpose(acc, tk_b),), b_layer=l)
        (gates,) = _matmul("in_proj_gates", h, *full, w_gate, 512,
                           [(jax.ShapeDtypeStruct((s, 2 * d), BF16), (tm, 512), lambda i, j: (i, j))], ident,
                           b_layer=l)

        def q_epilogue(acc, t_ref):
            rot = _rope_128(acc[:, LANES:], t_ref[...])
            return ((jnp.concatenate([acc[:, :LANES], rot], axis=1) * c_a).T,)

        (qt_a,) = _matmul("q_up_proj", lat, (tm, Q_LORA), lambda i, j: (i, 0), wq, hd,
                          [(jax.ShapeDtypeStruct((HA, hd, s), BF16), (None, hd, tm), lambda i, j: (j, 0, i))],
                          q_epilogue, gain=q_norm[l], extras=[(table, (tm, LANES), lambda i, j: (i, 0))])

        def kv_epilogue(acc, kr_ref, t_ref):
            rot = _rope_128(kr_ref[...].astype(F32), t_ref[...])
            lane = lax.broadcasted_iota(I32, rot.shape, 1)
            rot = jnp.where(lane < QK_ROPE, rot, 0.0)
            return jnp.concatenate([acc[:, :LANES], rot], axis=1), chunked_transpose(acc[:, LANES:], tk_a)

        k_a, vt_a = _matmul("kv_up_proj", lat, (tm, KV_LORA), lambda i, j: (i, Q_LORA // KV_LORA), wkv, hd,
                            [(jax.ShapeDtypeStruct((HA, s, hd), BF16), (None, tm, hd), lambda i, j: (j, i, 0)),
                             (jax.ShapeDtypeStruct((HA, s // tk_a, V_DIM_A + ONES_ROWS, tk_a), BF16),
                              (None, tm // tk_a, V_DIM_A + ONES_ROWS, tk_a), lambda i, j: (j, i, 0, 0))],
                            kv_epilogue, gain=kv_norm[l],
                            extras=[(lat, (tm, LANES), lambda i, j: (i, (Q_LORA + KV_LORA) // LANES)),
                                    (table, (tm, LANES), lambda i, j: (i, 0))])
        o_a = _mla_attention(qt_a, k_a, vt_a)

        lam_init = 0.8 - 0.6 * math.exp(-0.3 * l)
        lam_vecs = [v[l].reshape(1, DH_B).astype(F32) for v in (lam_q1, lam_k1, lam_q2, lam_k2)]
        o_b = _diff_attention(qt_b, k_b, vt_b, pos_col_f, pos_row_f, slopes_l2, lam_vecs, subln[l], lam_init)

        merged = _merge(o_a, o_b, w_oa[l].astype(BF16), w_ob[l].astype(BF16), gates)

        def residual(acc, x_ref):
            return (x_ref[...] + acc,)

        (x,) = _matmul("out_proj", merged, (tm, d), lambda i, j: (i, 0), w_out[l].astype(BF16), 512,
                       [(jax.ShapeDtypeStruct((s, d), F32), (tm, 512), lambda i, j: (i, j))], residual,
                       extras=[(x, (tm, 512), lambda i, j: (i, j))])

        ids, gate_vals, hp, counts = _router(x, norm_ffn[l], w_r_hi, w_r_lo, b_r)
        dest, blk_exp, nb_used, last_blk_row = _dispatch_plan(ids, counts, s)
        xs = _dispatch(hp, dest, last_blk_row, nb_used, blk_exp.shape[0])
        ys = _experts(xs, blk_exp + l * n_exp, nb_used, w1b, w3b, w2b)
        final = l == depth - 1
        x = _combine(x, ys, dest, gate_vals, norm_final if final else norm_ffn[l], final)
    return x.reshape(bsz, s, d)
```

```python
import functools
import math

import jax
import jax.numpy as jnp
import numpy as np
from jax import lax
from jax.experimental import pallas as pl
from jax.experimental.pallas import tpu as pltpu

F32 = jnp.float32
BF16 = jnp.bfloat16
I32 = jnp.int32
U32 = jnp.uint32

EPS = 1e-6
ROPE_THETA = 10000.0
HA, QK_NOPE, QK_ROPE, V_DIM_A = 16, 128, 64, 128
Q_LORA, KV_LORA = 1024, 512
HB, DH_B = 8, 128
N_GROUPS, EXP_PER_GROUP, TOP_K = 8, 8, 2
N_EXPERTS = N_GROUPS * EXP_PER_GROUP
LOG2E = 1.4426950408889634
NEG = -1e30

LANES = 128
MOE_ROWS = 128
ATT_TK_A = 1024
ATT_TK_B = 512
ATT_TQ = 1024
ATT_UNROLL = 2
ROW_DMA_UNROLL = 8
ONES_ROWS = 16
MIB = 1024 * 1024


def _cparams(semantics, vmem_mib):
    return pltpu.CompilerParams(dimension_semantics=semantics, vmem_limit_bytes=vmem_mib * MIB)


def _rms(xf, gain):
    ms = jnp.mean(xf * xf, axis=-1, keepdims=True)
    return xf * lax.rsqrt(ms + EPS) * gain


def _rmsnorm_body(x_ref, g_ref, o_ref):
    o_ref[...] = _rms(x_ref[...], g_ref[...]).astype(o_ref.dtype)


def _rmsnorm(x, gain, out_dtype, tm=256):
    s, d = x.shape
    tm = min(tm, s)
    return pl.pallas_call(
        _rmsnorm_body,
        grid=(s // tm,),
        in_specs=[pl.BlockSpec((tm, d), lambda i: (i, 0)), pl.BlockSpec((1, d), lambda i: (0, 0))],
        out_specs=pl.BlockSpec((tm, d), lambda i: (i, 0)),
        out_shape=jax.ShapeDtypeStruct((s, d), out_dtype),
        compiler_params=_cparams(("parallel",), 32),
        name="rmsnorm",
    )(x, gain.reshape(1, d))


def _matmul_body(*refs, n_extra, n_out, norm, epilogue):
    a_ref = refs[0]
    pos = 1
    if norm:
        g_ref = refs[pos]
        pos += 1
    b_ref = refs[pos]
    pos += 1
    extras = refs[pos:pos + n_extra]
    pos += n_extra
    outs = refs[pos:pos + n_out]
    pos += n_out
    if norm:
        lhs_ref = refs[pos]

        @pl.when(pl.program_id(1) == 0)
        def _():
            lhs_ref[...] = _rms(a_ref[...].astype(F32), g_ref[...]).astype(BF16)

        lhs = lhs_ref[...]
    else:
        lhs = a_ref[...]
    acc = jnp.dot(lhs, b_ref[...], preferred_element_type=F32)
    for o_ref, r in zip(outs, epilogue(acc, *extras)):
        o_ref[...] = r.astype(o_ref.dtype)


def _matmul(name, a, a_block, a_index, b, tn, outs, epilogue, gain=None, extras=(), vmem_mib=48, b_layer=None):
    tm, k = a_block
    m = a.shape[0]
    n = b.shape[-1]
    norm = gain is not None
    in_specs = [pl.BlockSpec(a_block, a_index)]
    args = [a]
    if norm:
        in_specs.append(pl.BlockSpec((1, k), lambda i, j: (0, 0)))
        args.append(gain.reshape(1, k).astype(F32))
    if b_layer is None:
        in_specs.append(pl.BlockSpec((k, tn), lambda i, j: (0, j)))
    else:
        in_specs.append(pl.BlockSpec((None, k, tn), lambda i, j: (b_layer, 0, j)))
    args.append(b)
    for arr, blk, imap in extras:
        in_specs.append(pl.BlockSpec(blk, imap))
        args.append(arr)
    res = pl.pallas_call(
        functools.partial(_matmul_body, n_extra=len(extras), n_out=len(outs), norm=norm, epilogue=epilogue),
        grid=(m // tm, n // tn),
        in_specs=in_specs,
        out_specs=[pl.BlockSpec(blk, imap) for _, blk, imap in outs],
        out_shape=[sds for sds, _, _ in outs],
        scratch_shapes=[pltpu.VMEM((tm, k), BF16)] if norm else [],
        compiler_params=_cparams(("parallel", "arbitrary"), vmem_mib),
        name=name,
    )(*args)
    return res


def _rope_table_body(pos_ref, inv_ref, t_ref):
    ang = pos_ref[...].astype(F32) * inv_ref[...]
    c = jnp.cos(ang)
    s = jnp.sin(ang)
    lane = lax.broadcasted_iota(I32, ang.shape, 1)
    t_ref[...] = jnp.where(lane < 2 * (QK_ROPE // 2), c, jnp.where(lane < 3 * (QK_ROPE // 2), -s, s))


def _rope_table(positions_col, tm=512):
    s = positions_col.shape[0]
    tm = min(tm, s)
    half = QK_ROPE // 2
    inv = ROPE_THETA ** (-jnp.arange(half, dtype=F32) / half)
    inv4 = jnp.tile(inv, 4).reshape(1, LANES)
    return pl.pallas_call(
        _rope_table_body,
        grid=(s // tm,),
        in_specs=[pl.BlockSpec((tm, 1), lambda i: (i, 0)), pl.BlockSpec((1, LANES), lambda i: (0, 0))],
        out_specs=pl.BlockSpec((tm, LANES), lambda i: (i, 0)),
        out_shape=jax.ShapeDtypeStruct((s, LANES), F32),
        compiler_params=_cparams(("parallel",), 16),
        name="rope_table",
    )(positions_col, inv4)


def _rope_128(v, table):
    t = v * table
    return t + pltpu.roll(t, 2 * (QK_ROPE // 2), axis=1)


def _softmax_step_t(st, vt, m_ref, acc_ref):
    m_prev = m_ref[...]
    m_new = jnp.maximum(m_prev, jnp.max(st, axis=0, keepdims=True))
    alpha = jnp.exp2(m_prev - m_new)
    p = jnp.exp2(st - m_new)
    acc_ref[...] = acc_ref[...] * alpha + jnp.dot(vt, p.astype(BF16), preferred_element_type=F32)
    m_ref[...] = m_new


def _causal_bias(tk, tq):
    key = jnp.arange(tq, dtype=I32).reshape(tq // tk, tk, 1)
    qry = jnp.arange(tq, dtype=I32).reshape(1, 1, tq)
    return jnp.where(key <= qry, 0.0, NEG).astype(F32)


def _init_stats(m_ref, acc_ref):
    m_ref[...] = jnp.full(m_ref.shape, NEG, F32)
    acc_ref[...] = jnp.zeros(acc_ref.shape, F32)


def _normalized(acc_ref, chain, dv):
    return acc_ref[chain, :dv, :] / acc_ref[chain, dv:dv + 1, :]


def _causal_pipeline(n_common, n_diag, chains, scores, update, s_ref, unroll=ATT_UNROLL):
    assert unroll % 2 == 0

    def ahead(c, slot):
        for a, st in zip(chains, scores(c)):
            s_ref[2 * a + slot] = st

    def fold(c, slot, d=None):
        for a in chains:
            update(c, a, s_ref[2 * a + slot], d)

    def step(c, t):
        ahead(c + 1, (t + 1) % 2)
        fold(c, t % 2)

    ahead(0, 0)
    n_groups = n_common // unroll

    def body(g, carry):
        for t in range(unroll):
            step(g * unroll + t, t)
        return carry

    lax.fori_loop(0, n_groups, body, 0)
    c0 = n_groups * unroll
    rem = n_common - c0
    for t in range(unroll - 1):
        @pl.when(t < rem)
        def _(t=t):
            step(c0 + t, t)

    for parity in range(2):
        @pl.when(rem % 2 == parity)
        def _(parity=parity):
            for d in range(n_diag):
                if d + 1 < n_diag:
                    ahead(n_common + d + 1, (parity + d + 1) % 2)
                fold(n_common + d, (parity + d) % 2, d)


def _mla_body(qt_ref, k_ref, vt_ref, mask_ref, o_ref, m_ref, acc_ref, s_ref, *, tq, tk):
    i = pl.program_id(1)
    _init_stats(m_ref, acc_ref)
    qt = qt_ref[...]

    def scores(c):
        return [jnp.dot(k_ref[pl.ds(pl.multiple_of(c * tk, tk), tk), :], qt, preferred_element_type=F32)]

    def update(c, a, st, d):
        if d is not None:
            st = st + mask_ref[d]
        _softmax_step_t(st, vt_ref[c], m_ref.at[a], acc_ref.at[a])

    _causal_pipeline(i * (tq // tk), tq // tk, [0], scores, update, s_ref)
    o_ref[...] = _normalized(acc_ref, 0, o_ref.shape[1]).T.astype(o_ref.dtype)


def _mla_attention(qt, k, vt):
    h, s, dk = k.shape
    _, nchunk, dvp, tk = vt.shape
    dv = dvp - ONES_ROWS
    tq = max(min(ATT_TQ, s), tk)
    return pl.pallas_call(
        functools.partial(_mla_body, tq=tq, tk=tk),
        grid=(h, s // tq),
        in_specs=[
            pl.BlockSpec((None, dk, tq), lambda hh, i: (hh, 0, i)),
            pl.BlockSpec((None, s, dk), lambda hh, i: (hh, 0, 0)),
            pl.BlockSpec((None, nchunk, dvp, tk), lambda hh, i: (hh, 0, 0, 0)),
            pl.BlockSpec((tq // tk, tk, tq), lambda hh, i: (0, 0, 0), pipeline_mode=pl.Buffered(1)),
        ],
        out_specs=pl.BlockSpec((tq, dv), lambda hh, i: (i, hh)),
        out_shape=jax.ShapeDtypeStruct((s, h * dv), BF16),
        scratch_shapes=[pltpu.VMEM((1, 1, tq), F32), pltpu.VMEM((1, dvp, tq), F32), pltpu.VMEM((2, tk, tq), F32)],
        compiler_params=_cparams(("parallel", "arbitrary"), 48),
        name="mla_attention",
    )(qt, k, vt, _causal_bias(tk, tq))


def _diff_body(slope_ref, qt_ref, k_ref, vt_ref, mask_ref, pcol_ref, prow_ref, lq1_ref, lk1_ref, lq2_ref, lk2_ref,
               sub_ref, o_ref, m_ref, acc_ref, s_ref, *, tq, tk, lam_init):
    h = pl.program_id(0)
    i = pl.program_id(1)
    _init_stats(m_ref, acc_ref)
    qts = [qt_ref[c * DH_B:(c + 1) * DH_B, :] for c in range(2)]
    pq = prow_ref[...]
    slope = slope_ref[h]

    def scores(c):
        start = pl.multiple_of(c * tk, tk)
        k = k_ref[pl.ds(start, tk), :]
        bias = jnp.abs(pcol_ref[pl.ds(start, tk), :] - pq) * slope
        return [jnp.dot(k[:, mp * DH_B:(mp + 1) * DH_B], qts[mp], preferred_element_type=F32) - bias
                for mp in range(2)]

    def update(c, chain, st, d):
        if d is not None:
            st = st + mask_ref[d]
        _softmax_step_t(st, vt_ref[c], m_ref.at[chain], acc_ref.at[chain])

    _causal_pipeline(i * (tq // tk), tq // tk, [0, 1], scores, update, s_ref)

    lam = (jnp.exp(jnp.sum(lq1_ref[...] * lk1_ref[...], axis=1, keepdims=True))
           - jnp.exp(jnp.sum(lq2_ref[...] * lk2_ref[...], axis=1, keepdims=True)) + lam_init)
    dv = o_ref.shape[1]
    o = _normalized(acc_ref, 0, dv) - lam * _normalized(acc_ref, 1, dv)
    ms = jnp.mean(o * o, axis=0, keepdims=True)
    y = o * lax.rsqrt(ms + EPS) * sub_ref[...] * (1.0 - lam_init)
    o_ref[...] = y.T.astype(o_ref.dtype)


def _diff_attention(qt, k, vt, pos_col, pos_row, slopes_l2, lam_vecs, subln, lam_init):
    _, s, dqk = k.shape
    _, nchunk, dvp, tk = vt.shape
    dv = dvp - ONES_ROWS
    tq = max(min(ATT_TQ, s), tk)
    vec = pl.BlockSpec((1, DH_B), lambda hh, i: (0, 0))
    once = pl.Buffered(1)
    return pl.pallas_call(
        functools.partial(_diff_body, tq=tq, tk=tk, lam_init=lam_init),
        grid=(HB, s // tq),
        in_specs=[
            pl.BlockSpec(memory_space=pltpu.SMEM),
            pl.BlockSpec((None, dqk, tq), lambda hh, i: (hh, 0, i)),
            pl.BlockSpec((None, s, dqk), lambda hh, i: (hh, 0, 0), pipeline_mode=once),
            pl.BlockSpec((None, nchunk, dvp, tk), lambda hh, i: (hh, 0, 0, 0), pipeline_mode=once),
            pl.BlockSpec((tq // tk, tk, tq), lambda hh, i: (0, 0, 0), pipeline_mode=once),
            pl.BlockSpec((s, 1), lambda hh, i: (0, 0), pipeline_mode=once),
            pl.BlockSpec((1, tq), lambda hh, i: (0, i)),
            vec, vec, vec, vec,
            pl.BlockSpec((dv, 1), lambda hh, i: (0, 0)),
        ],
        out_specs=pl.BlockSpec((tq, dv), lambda hh, i: (i, hh)),
        out_shape=jax.ShapeDtypeStruct((s, HB * dv), BF16),
        scratch_shapes=[pltpu.VMEM((2, 1, tq), F32), pltpu.VMEM((2, dvp, tq), F32), pltpu.VMEM((4, tk, tq), F32)],
        compiler_params=_cparams(("parallel", "arbitrary"), 56),
        name="diff_attention",
    )(slopes_l2, qt, k, vt, _causal_bias(tk, tq), pos_col, pos_row, *lam_vecs, subln.reshape(dv, 1))


def _merge_body(oa_ref, ob_ref, woa_ref, wob_ref, ga_ref, gb_ref, o_ref):
    ya = jnp.dot(oa_ref[...], woa_ref[...], preferred_element_type=F32)
    yb = jnp.dot(ob_ref[...], wob_ref[...], preferred_element_type=F32)
    out = jax.nn.sigmoid(ga_ref[...].astype(F32)) * ya + jax.nn.sigmoid(gb_ref[...].astype(F32)) * yb
    o_ref[...] = out.astype(o_ref.dtype)


def _merge(o_a, o_b, w_oa, w_ob, gates, tm=1024, tn=512):
    s, ka = o_a.shape
    kb = o_b.shape[1]
    d = w_oa.shape[1]
    tm = min(tm, s)
    nj = d // tn
    return pl.pallas_call(
        _merge_body,
        grid=(s // tm, nj),
        in_specs=[
            pl.BlockSpec((tm, ka), lambda i, j: (i, 0)),
            pl.BlockSpec((tm, kb), lambda i, j: (i, 0)),
            pl.BlockSpec((ka, tn), lambda i, j: (0, j)),
            pl.BlockSpec((kb, tn), lambda i, j: (0, j)),
            pl.BlockSpec((tm, tn), lambda i, j: (i, j)),
            pl.BlockSpec((tm, tn), lambda i, j: (i, nj + j)),
        ],
        out_specs=pl.BlockSpec((tm, tn), lambda i, j: (i, j)),
        out_shape=jax.ShapeDtypeStruct((s, d), BF16),
        compiler_params=_cparams(("parallel", "arbitrary"), 48),
        name="gated_merge",
    )(o_a, o_b, w_oa, w_ob, gates, gates)


def _pack_pairs(x):
    half = x.shape[1] // 2
    lo = pltpu.bitcast(x[:, :half].astype(BF16).astype(F32), U32)
    hi = pltpu.bitcast(x[:, half:].astype(BF16).astype(F32), U32)
    return (lo >> 16) | (hi & jnp.uint32(0xFFFF0000))


def _unpack_pairs(p):
    lo = pltpu.bitcast(p << 16, F32)
    hi = pltpu.bitcast(p & jnp.uint32(0xFFFF0000), F32)
    return jnp.concatenate([lo, hi], axis=1)


def _router_body(x_ref, g_ref, whi_ref, wlo_ref, b_ref, tri_ref, ids_ref, gates_ref, hp_ref, cnt_ref):
    @pl.when(pl.program_id(0) == 0)
    def _():
        cnt_ref[...] = jnp.zeros(cnt_ref.shape, F32)

    h = _rms(x_ref[...], g_ref[...])
    hp_ref[...] = _pack_pairs(h)
    h_hi = h.astype(BF16)
    h_lo = (h - h_hi.astype(F32)).astype(BF16)
    logits = (jnp.dot(h_hi, whi_ref[...], preferred_element_type=F32)
              + jnp.dot(h_lo, whi_ref[...], preferred_element_type=F32)
              + jnp.dot(h_hi, wlo_ref[...], preferred_element_type=F32)) + b_ref[...]
    lane = lax.broadcasted_iota(I32, logits.shape, 1).astype(F32)

    def first_argmax(vals, mx):
        return jnp.min(jnp.where(vals == mx, lane, 4.0 * LANES), axis=1, keepdims=True)

    is_g = jnp.logical_and(lane >= N_EXPERTS, lane < N_EXPERTS + N_GROUPS)
    lg = jnp.where(is_g, logits, NEG)
    mg = jnp.max(lg, axis=1, keepdims=True)
    p_g = 1.0 / jnp.sum(jnp.exp(lg - mg), axis=1, keepdims=True)
    g_idx = first_argmax(lg, mg) - N_EXPERTS
    lo = g_idx * EXP_PER_GROUP
    in_grp = jnp.logical_and(lane >= lo, lane < lo + EXP_PER_GROUP)
    le = jnp.where(in_grp, logits, NEG)
    m1 = jnp.max(le, axis=1, keepdims=True)
    i1 = first_argmax(le, m1)
    den = jnp.sum(jnp.exp(le - m1), axis=1, keepdims=True)
    le2 = jnp.where(lane == i1, NEG, le)
    m2 = jnp.max(le2, axis=1, keepdims=True)
    i2 = first_argmax(le2, m2)
    p1 = 1.0 / den
    p2 = jnp.exp(m2 - m1) / den
    psum = p1 + p2
    gate1 = p_g * p1 / psum
    gate2 = p_g * p2 / psum

    hot1 = jnp.where(lane == i1, 1.0, 0.0)
    hot2 = jnp.where(lane == i2, 1.0, 0.0)
    tri = tri_ref[...]
    before1 = jnp.dot(tri, hot1.astype(BF16), preferred_element_type=F32) + cnt_ref[...]
    tot1 = jnp.sum(hot1, axis=0, keepdims=True)
    before2 = jnp.dot(tri, hot2.astype(BF16), preferred_element_type=F32) + (cnt_ref[...] + tot1)
    r1 = jnp.sum(hot1 * before1, axis=1, keepdims=True)
    r2 = jnp.sum(hot2 * before2, axis=1, keepdims=True)
    cnt_ref[...] = cnt_ref[...] + tot1 + jnp.sum(hot2, axis=0, keepdims=True)

    ids = jnp.where(lane == 0, i1, jnp.where(lane == 1, i2, jnp.where(lane == 2, r1, jnp.where(lane == 3, r2, 0.0))))
    ids_ref[...] = ids.astype(I32)
    gates_ref[...] = jnp.where(lane == 0, gate1, jnp.where(lane == 1, gate2, 0.0))


def _router(x, gain, w_hi, w_lo, bias, tm=256):
    t, d = x.shape
    tm = min(tm, t)
    tri = jnp.tril(jnp.ones((tm, tm), F32), -1).astype(BF16)
    return pl.pallas_call(
        _router_body,
        grid=(t // tm,),
        in_specs=[
            pl.BlockSpec((tm, d), lambda i: (i, 0)),
            pl.BlockSpec((1, d), lambda i: (0, 0)),
            pl.BlockSpec((d, LANES), lambda i: (0, 0)),
            pl.BlockSpec((d, LANES), lambda i: (0, 0)),
            pl.BlockSpec((1, LANES), lambda i: (0, 0)),
            pl.BlockSpec((tm, tm), lambda i: (0, 0)),
        ],
        out_specs=[pl.BlockSpec((tm, LANES), lambda i: (i, 0)), pl.BlockSpec((tm, LANES), lambda i: (i, 0)),
                   pl.BlockSpec((tm, d // 2), lambda i: (i, 0)), pl.BlockSpec((1, LANES), lambda i: (0, 0))],
        out_shape=[jax.ShapeDtypeStruct((t, LANES), I32), jax.ShapeDtypeStruct((t, LANES), F32),
                   jax.ShapeDtypeStruct((t, d // 2), U32), jax.ShapeDtypeStruct((1, LANES), F32)],
        compiler_params=_cparams(("arbitrary",), 40),
        name="moe_router",
    )(x, gain.reshape(1, d), w_hi, w_lo, bias, tri)


def _dispatch_plan(ids, counts, t):
    rows = MOE_ROWS
    nb = t * TOP_K // rows + N_EXPERTS
    e = ids[:, :TOP_K]
    rank = ids[:, TOP_K:2 * TOP_K]
    cnt = counts[0, :N_EXPERTS].astype(I32)
    padded = (cnt + rows - 1) // rows * rows
    experts = jnp.arange(N_EXPERTS, dtype=I32)
    pad_end = jnp.sum(jnp.where(experts[None, :] <= experts[:, None], padded[None, :], 0), axis=1).astype(I32)
    pad_start = pad_end - padded
    hot = e[:, :, None] == experts
    dest = jnp.sum(jnp.where(hot, pad_start, 0), axis=-1).astype(I32) + rank
    nb_used = (pad_end[-1:] // rows).astype(I32)
    blk_row = jnp.arange(nb, dtype=I32) * rows
    blk_exp = jnp.minimum(jnp.sum(pad_end[None, :] <= blk_row[:, None], axis=1), N_EXPERTS - 1).astype(I32)
    last_blk_row = jnp.where(cnt > 0, pad_end - rows, -1).astype(I32)
    return dest, blk_exp, nb_used, last_blk_row


def _dispatch_body(last_ref, nbu_ref, dest_ref, hp_ref, xs_hbm, zbuf, sem_z, sem, *, tm, rows, nb):
    i = pl.program_id(0)

    @pl.when(i == 0)
    def _():
        zbuf[...] = jnp.zeros(zbuf.shape, U32)

        def expert_fill(e):
            return pltpu.make_async_copy(zbuf, xs_hbm.at[pl.ds(pl.multiple_of(last_ref[e], rows), rows)], sem_z)

        def tail_fill(b):
            return pltpu.make_async_copy(zbuf, xs_hbm.at[pl.ds(pl.multiple_of(b * rows, rows), rows)], sem_z)

        for wait in (False, True):
            def experts(e, c, wait=wait):
                @pl.when(last_ref[e] >= 0)
                def _():
                    expert_fill(e).wait() if wait else expert_fill(e).start()
                return c

            def tail(b, c, wait=wait):
                @pl.when(b >= nbu_ref[0])
                def _():
                    tail_fill(b).wait() if wait else tail_fill(b).start()
                return c

            lax.fori_loop(0, N_EXPERTS, experts, 0)
            lax.fori_loop(0, nb, tail, 0)

    def start_rows(r, c):
        for k in range(TOP_K):
            pltpu.make_async_copy(hp_ref.at[pl.ds(r, 1)], xs_hbm.at[pl.ds(dest_ref[0, TOP_K * r + k], 1)], sem).start()
        return c

    lax.fori_loop(0, tm, start_rows, 0, unroll=ROW_DMA_UNROLL)
    whole = xs_hbm.at[pl.ds(0, TOP_K * tm)]
    pltpu.make_async_copy(whole, whole, sem).wait()


def _dispatch(hp, dest, last_blk_row, nb_used, nb, tm=256):
    t, w = hp.shape
    tm = min(tm, t)
    rows = MOE_ROWS
    grid_spec = pltpu.PrefetchScalarGridSpec(
        num_scalar_prefetch=2,
        grid=(t // tm,),
        in_specs=[pl.BlockSpec((None, 1, TOP_K * tm), lambda i, lr, nu: (i, 0, 0), memory_space=pltpu.SMEM),
                  pl.BlockSpec((tm, w), lambda i, lr, nu: (i, 0))],
        out_specs=pl.BlockSpec(memory_space=pl.ANY),
        scratch_shapes=[pltpu.VMEM((rows, w), U32), pltpu.SemaphoreType.DMA(()), pltpu.SemaphoreType.DMA(())],
    )
    return pl.pallas_call(
        functools.partial(_dispatch_body, tm=tm, rows=rows, nb=nb),
        grid_spec=grid_spec,
        out_shape=jax.ShapeDtypeStruct((nb * rows, w), U32),
        compiler_params=_cparams(("arbitrary",), 16),
        name="moe_dispatch",
    )(last_blk_row, nb_used, dest.reshape(t // tm, 1, TOP_K * tm), hp)


def _expert_body(blk_exp_ref, nbu_ref, xs_ref, w1_ref, w3_ref, w2_ref, ys_ref):
    del blk_exp_ref
    b = pl.program_id(0)

    @pl.when(b < nbu_ref[0])
    def _():
        xn = _unpack_pairs(xs_ref[...]).astype(BF16)
        a1 = jnp.dot(xn, w1_ref[...], preferred_element_type=F32)
        a3 = jnp.dot(xn, w3_ref[...], preferred_element_type=F32)
        hmid = (jax.nn.silu(a1) * a3).astype(BF16)
        ys_ref[...] = _pack_pairs(jnp.dot(hmid, w2_ref[...], preferred_element_type=F32))

    @pl.when(b >= nbu_ref[0])
    def _():
        ys_ref[...] = jnp.zeros(ys_ref.shape, U32)


def _experts(xs, blk_exp, nb_used, w1, w3, w2):
    p, w = xs.shape
    _, d, de = w1.shape
    rows = MOE_ROWS
    grid_spec = pltpu.PrefetchScalarGridSpec(
        num_scalar_prefetch=2,
        grid=(p // rows,),
        in_specs=[
            pl.BlockSpec((rows, w), lambda b, be, nu: (jnp.minimum(b, nu[0] - 1), 0)),
            pl.BlockSpec((None, d, de), lambda b, be, nu: (be[b], 0, 0)),
            pl.BlockSpec((None, d, de), lambda b, be, nu: (be[b], 0, 0)),
            pl.BlockSpec((None, de, d), lambda b, be, nu: (be[b], 0, 0)),
        ],
        out_specs=pl.BlockSpec((rows, w), lambda b, be, nu: (b, 0)),
    )
    return pl.pallas_call(
        _expert_body,
        grid_spec=grid_spec,
        out_shape=jax.ShapeDtypeStruct((p, w), U32),
        compiler_params=_cparams(("arbitrary",), 40),
        name="moe_experts",
    )(blk_exp, nb_used, xs, w1, w3, w2)


def _combine_body(dest_ref, dest_next_ref, x_ref, gates_ref, g_ref, ys_hbm, o_ref, ybuf, sem, *, tm, final):
    i = pl.program_id(0)
    slot = lax.rem(i, 2)

    def start_rows(d_ref, s):
        def rows_loop(r, c):
            for k in range(TOP_K):
                pltpu.make_async_copy(ys_hbm.at[pl.ds(d_ref[0, TOP_K * r + k], 1)], ybuf.at[s, k, pl.ds(r, 1)],
                                      sem.at[s]).start()
            return c

        lax.fori_loop(0, tm, rows_loop, 0, unroll=ROW_DMA_UNROLL)

    @pl.when(i == 0)
    def _():
        start_rows(dest_ref, 0)

    @pl.when(i + 1 < pl.num_programs(0))
    def _():
        start_rows(dest_next_ref, 1 - slot)

    pltpu.make_async_copy(ybuf.at[slot], ybuf.at[slot], sem.at[slot]).wait()
    gates = gates_ref[...]
    x = x_ref[...] + (gates[:, 0:1] * _unpack_pairs(ybuf[slot, 0]) + gates[:, 1:2] * _unpack_pairs(ybuf[slot, 1]))
    o_ref[...] = _rms(x, g_ref[...]) if final else x


def _combine(x, ys, dest, gates, gain, final, tm=256):
    t, d = x.shape
    tm = min(tm, t)
    w = ys.shape[1]
    n = t // tm
    dest3 = dest.reshape(n, 1, TOP_K * tm)
    return pl.pallas_call(
        functools.partial(_combine_body, tm=tm, final=final),
        grid=(n,),
        in_specs=[pl.BlockSpec((None, 1, TOP_K * tm), lambda i: (i, 0, 0), memory_space=pltpu.SMEM),
                  pl.BlockSpec((None, 1, TOP_K * tm), lambda i: (jnp.minimum(i + 1, n - 1), 0, 0),
                               memory_space=pltpu.SMEM),
                  pl.BlockSpec((tm, d), lambda i: (i, 0)),
                  pl.BlockSpec((tm, LANES), lambda i: (i, 0)),
                  pl.BlockSpec((1, d), lambda i: (0, 0)),
                  pl.BlockSpec(memory_space=pl.ANY)],
        out_specs=pl.BlockSpec((tm, d), lambda i: (i, 0)),
        out_shape=jax.ShapeDtypeStruct((t, d), F32),
        scratch_shapes=[pltpu.VMEM((2, TOP_K, tm, w), U32), pltpu.SemaphoreType.DMA((2,))],
        compiler_params=_cparams(("arbitrary",), 48),
        name="moe_combine",
    )(dest3, dest3, x, gates, gain.reshape(1, d), ys)


def _split_w_in_body(wt_ref, lat_ref, qb_ref, kb_ref, vb_ref, gate_ref):
    wt = wt_ref[...]
    tr = wt.shape[1]
    d = gate_ref.shape[1] // 2
    o = np.cumsum([0, Q_LORA, KV_LORA, QK_ROPE, HB * 2 * DH_B, HB * 2 * DH_B, HB * 2 * DH_B, 2 * d])
    half = QK_ROPE // 2
    x1, x2 = wt[o[2]:o[2] + half], wt[o[2] + half:o[3]]
    pad = jnp.zeros((lat_ref.shape[1] - Q_LORA - KV_LORA - LANES, tr), wt.dtype)
    lat_ref[...] = jnp.concatenate([wt[:o[2]], x1, x2, x2, x1, pad], axis=0).T.astype(BF16)
    qb_ref[...] = wt[o[3]:o[4]].T.astype(BF16)
    kb_ref[...] = wt[o[4]:o[5]].T.astype(BF16)
    vb_ref[...] = wt[o[5]:o[6]].T.astype(BF16)
    gate_ref[...] = wt[o[6]:o[7]].T.astype(BF16)


def _split_w_in(w_in, tr=128):
    layers, d, d_in = w_in.shape
    wb = HB * 2 * DH_B
    widths = (2048, wb, wb, wb, 2 * d)
    return pl.pallas_call(
        _split_w_in_body,
        grid=(layers, d // tr),
        in_specs=[pl.BlockSpec((None, d_in, tr), lambda l, i: (l, 0, i))],
        out_specs=[pl.BlockSpec((None, tr, wd), lambda l, i: (l, i, 0)) for wd in widths],
        out_shape=[jax.ShapeDtypeStruct((layers, d, wd), BF16) for wd in widths],
        compiler_params=_cparams(("parallel", "parallel"), 48),
        name="split_w_in",
    )(jnp.swapaxes(w_in, 1, 2))


def _rope_cols(w_rope):
    half = QK_ROPE // 2
    x1, x2 = w_rope[..., :half], w_rope[..., half:]
    return jnp.concatenate([x1, x2, x2, x1], axis=-1)


def _layer_weights(w_uq, w_ukv, w_router_g, b_router_g, w_router_e, b_router_e):
    d = w_router_g.shape[0]
    wq = w_uq.reshape(Q_LORA, HA, QK_NOPE + QK_ROPE)
    wq = jnp.concatenate([wq[..., :QK_NOPE], _rope_cols(wq[..., QK_NOPE:])], axis=-1).reshape(Q_LORA, HA * 2 * LANES)
    w_r = jnp.concatenate([w_router_e, w_router_g, jnp.zeros((d, LANES - N_EXPERTS - N_GROUPS), F32)], axis=1)
    w_r_hi = w_r.astype(BF16)
    w_r_lo = (w_r - w_r_hi.astype(F32)).astype(BF16)
    b_r = jnp.concatenate([b_router_e, b_router_g, jnp.zeros((LANES - N_EXPERTS - N_GROUPS,), F32)]).reshape(1, LANES)
    return wq.astype(BF16), w_ukv.astype(BF16), w_r_hi, w_r_lo, b_r


def kernel(x, positions, norm_attn, w_in, q_norm, w_uq, kv_norm, w_ukv, lam_q1, lam_k1, lam_q2, lam_k2, subln, w_oa,
           w_ob, w_out, norm_ffn, w_router_g, b_router_g, w_router_e, b_router_e, w1, w3, w2, norm_final):
    bsz, s, d = x.shape
    assert bsz == 1
    depth = w_in.shape[0]
    x = x.reshape(s, d)
    pos_col = positions.reshape(s, 1)
    table = _rope_table(pos_col)
    pos_col_f = pos_col.astype(F32)
    pos_row_f = pos_col_f.reshape(1, s)
    slopes_l2 = jnp.asarray(2.0 ** (-8.0 * (np.arange(HB) + 1) / HB) * LOG2E, dtype=F32)
    c_a = (QK_NOPE + QK_ROPE) ** -0.5 * LOG2E
    c_b = DH_B ** -0.5 * LOG2E
    tm = min(1024, s)
    tk_a, tk_b = min(ATT_TK_A, s), min(ATT_TK_B, s)
    hd = 2 * LANES

    def chunked_transpose(acc, tk):
        t = jnp.concatenate([acc.T, jnp.ones((ONES_ROWS, tm), F32)], axis=0)
        return jnp.stack([t[:, c * tk:(c + 1) * tk] for c in range(tm // tk)])

    w_lat, w_qb, w_kb, w_vb, w_gate = _split_w_in(w_in)
    n_exp = w1.shape[1]
    w1b, w3b, w2b = (w.reshape((depth * n_exp,) + w.shape[2:]).astype(BF16) for w in (w1, w3, w2))

    for l in range(depth):
        wq, wkv, w_r_hi, w_r_lo, b_r = _layer_weights(
            w_uq[l], w_ukv[l], w_router_g[l], b_router_g[l], w_router_e[l], b_router_e[l])
        h = _rmsnorm(x, norm_attn[l], BF16)
        full = ((tm, d), lambda i, j: (i, 0))

        def ident(acc):
            return (acc,)

        (lat,) = _matmul("in_proj_latent", h, *full, w_lat, 512,
                         [(jax.ShapeDtypeStruct((s, 2048), BF16), (tm, 512), lambda i, j: (i, j))], ident, b_layer=l)

        (qt_b,) = _matmul("in_proj_q_b", h, *full, w_qb, hd,
                          [(jax.ShapeDtypeStruct((HB, hd, s), BF16), (None, hd, tm), lambda i, j: (j, 0, i))],
                          lambda acc: ((acc * c_b).T,), b_layer=l)
        (k_b,) = _matmul("in_proj_k_b", h, *full, w_kb, hd,
                         [(jax.ShapeDtypeStruct((HB, s, hd), BF16), (None, tm, hd), lambda i, j: (j, i, 0))], ident,
                         b_layer=l)
        (vt_b,) = _matmul("in_proj_v_b", h, *full, w_vb, hd,
                          [(jax.ShapeDtypeStruct((HB, s // tk_b, hd + ONES_ROWS, tk_b), BF16),
                            (None, tm // tk_b, hd + ONES_ROWS, tk_b),
                            lambda i, j: (j, i, 0, 0))], lambda acc: (chunked_transpose(acc, tk_b),), b_layer=l)
        (gates,) = _matmul("in_proj_gates", h, *full, w_gate, 512,
                           [(jax.ShapeDtypeStruct((s, 2 * d), BF16), (tm, 512), lambda i, j: (i, j))], ident,
                           b_layer=l)

        def q_epilogue(acc, t_ref):
            rot = _rope_128(acc[:, LANES:], t_ref[...])
            return ((jnp.concatenate([acc[:, :LANES], rot], axis=1) * c_a).T,)

        (qt_a,) = _matmul("q_up_proj", lat, (tm, Q_LORA), lambda i, j: (i, 0), wq, hd,
                          [(jax.ShapeDtypeStruct((HA, hd, s), BF16), (None, hd, tm), lambda i, j: (j, 0, i))],
                          q_epilogue, gain=q_norm[l], extras=[(table, (tm, LANES), lambda i, j: (i, 0))])

        def kv_epilogue(acc, kr_ref, t_ref):
            rot = _rope_128(kr_ref[...].astype(F32), t_ref[...])
            lane = lax.broadcasted_iota(I32, rot.shape, 1)
            rot = jnp.where(lane < QK_ROPE, rot, 0.0)
            return jnp.concatenate([acc[:, :LANES], rot], axis=1), chunked_transpose(acc[:, LANES:], tk_a)

        k_a, vt_a = _matmul("kv_up_proj", lat, (tm, KV_LORA), lambda i, j: (i, Q_LORA // KV_LORA), wkv, hd,
                            [(jax.ShapeDtypeStruct((HA, s, hd), BF16), (None, tm, hd), lambda i, j: (j, i, 0)),
                             (jax.ShapeDtypeStruct((HA, s // tk_a, V_DIM_A + ONES_ROWS, tk_a), BF16),
                              (None, tm // tk_a, V_DIM_A + ONES_ROWS, tk_a), lambda i, j: (j, i, 0, 0))],
                            kv_epilogue, gain=kv_norm[l],
                            extras=[(lat, (tm, LANES), lambda i, j: (i, (Q_LORA + KV_LORA) // LANES)),
                                    (table, (tm, LANES), lambda i, j: (i, 0))])
        o_a = _mla_attention(qt_a, k_a, vt_a)

        lam_init = 0.8 - 0.6 * math.exp(-0.3 * l)
        lam_vecs = [v[l].reshape(1, DH_B).astype(F32) for v in (lam_q1, lam_k1, lam_q2, lam_k2)]
        o_b = _diff_attention(qt_b, k_b, vt_b, pos_col_f, pos_row_f, slopes_l2, lam_vecs, subln[l], lam_init)

        merged = _merge(o_a, o_b, w_oa[l].astype(BF16), w_ob[l].astype(BF16), gates)

        def residual(acc, x_ref):
            return (x_ref[...] + acc,)

        (x,) = _matmul("out_proj", merged, (tm, d), lambda i, j: (i, 0), w_out[l].astype(BF16), 512,
                       [(jax.ShapeDtypeStruct((s, d), F32), (tm, 512), lambda i, j: (i, j))], residual,
                       extras=[(x, (tm, 512), lambda i, j: (i, j))])

        ids, gate_vals, hp, counts = _router(x, norm_ffn[l], w_r_hi, w_r_lo, b_r)
        dest, blk_exp, nb_used, last_blk_row = _dispatch_plan(ids, counts, s)
        xs = _dispatch(hp, dest, last_blk_row, nb_used, blk_exp.shape[0])
        ys = _experts(xs, blk_exp + l * n_exp, nb_used, w1b, w3b, w2b)
        final = l == depth - 1
        x = _combine(x, ys, dest, gate_vals, norm_final if final else norm_ffn[l], final)
    return x.reshape(bsz, s, d)
```

```python
import functools
import math

import jax
import jax.numpy as jnp
import numpy as np
from jax import lax
from jax.experimental import pallas as pl
from jax.experimental.pallas import tpu as pltpu

F32 = jnp.float32
BF16 = jnp.bfloat16
I32 = jnp.int32
U32 = jnp.uint32

EPS = 1e-6
ROPE_THETA = 10000.0
HA, QK_NOPE, QK_ROPE, V_DIM_A = 16, 128, 64, 128
Q_LORA, KV_LORA = 1024, 512
HB, DH_B = 8, 128
N_GROUPS, EXP_PER_GROUP, TOP_K = 8, 8, 2
N_EXPERTS = N_GROUPS * EXP_PER_GROUP
LOG2E = 1.4426950408889634
NEG = -1e30

LANES = 128
MOE_ROWS = 128
ATT_TK_A = 1024
ATT_TK_B = 512
ATT_TQ = 1024
ATT_UNROLL = 2
ROW_DMA_UNROLL = 8
ONES_ROWS = 16
MIB = 1024 * 1024


def _cparams(semantics, vmem_mib):
    return pltpu.CompilerParams(dimension_semantics=semantics, vmem_limit_bytes=vmem_mib * MIB)


def _rms(xf, gain):
    ms = jnp.mean(xf * xf, axis=-1, keepdims=True)
    return xf * lax.rsqrt(ms + EPS) * gain


def _rmsnorm_body(x_ref, g_ref, o_ref):
    o_ref[...] = _rms(x_ref[...], g_ref[...]).astype(o_ref.dtype)


def _rmsnorm(x, gain, out_dtype, tm=256):
    s, d = x.shape
    tm = min(tm, s)
    return pl.pallas_call(
        _rmsnorm_body,
        grid=(s // tm,),
        in_specs=[pl.BlockSpec((tm, d), lambda i: (i, 0)), pl.BlockSpec((1, d), lambda i: (0, 0))],
        out_specs=pl.BlockSpec((tm, d), lambda i: (i, 0)),
        out_shape=jax.ShapeDtypeStruct((s, d), out_dtype),
        compiler_params=_cparams(("parallel",), 32),
        name="rmsnorm",
    )(x, gain.reshape(1, d))


def _matmul_body(*refs, n_extra, n_out, norm, epilogue):
    a_ref = refs[0]
    pos = 1
    if norm:
        g_ref = refs[pos]
        pos += 1
    b_ref = refs[pos]
    pos += 1
    extras = refs[pos:pos + n_extra]
    pos += n_extra
    outs = refs[pos:pos + n_out]
    pos += n_out
    if norm:
        lhs_ref = refs[pos]

        @pl.when(pl.program_id(1) == 0)
        def _():
            lhs_ref[...] = _rms(a_ref[...].astype(F32), g_ref[...]).astype(BF16)

        lhs = lhs_ref[...]
    else:
        lhs = a_ref[...]
    acc = jnp.dot(lhs, b_ref[...], preferred_element_type=F32)
    for o_ref, r in zip(outs, epilogue(acc, *extras)):
        o_ref[...] = r.astype(o_ref.dtype)


def _matmul(name, a, a_block, a_index, b, tn, outs, epilogue, gain=None, extras=(), vmem_mib=48, b_layer=None):
    tm, k = a_block
    m = a.shape[0]
    n = b.shape[-1]
    norm = gain is not None
    in_specs = [pl.BlockSpec(a_block, a_index)]
    args = [a]
    if norm:
        in_specs.append(pl.BlockSpec((1, k), lambda i, j: (0, 0)))
        args.append(gain.reshape(1, k).astype(F32))
    if b_layer is None:
        in_specs.append(pl.BlockSpec((k, tn), lambda i, j: (0, j)))
    else:
        in_specs.append(pl.BlockSpec((None, k, tn), lambda i, j: (b_layer, 0, j)))
    args.append(b)
    for arr, blk, imap in extras:
        in_specs.append(pl.BlockSpec(blk, imap))
        args.append(arr)
    res = pl.pallas_call(
        functools.partial(_matmul_body, n_extra=len(extras), n_out=len(outs), norm=norm, epilogue=epilogue),
        grid=(m // tm, n // tn),
        in_specs=in_specs,
        out_specs=[pl.BlockSpec(blk, imap) for _, blk, imap in outs],
        out_shape=[sds for sds, _, _ in outs],
        scratch_shapes=[pltpu.VMEM((tm, k), BF16)] if norm else [],
        compiler_params=_cparams(("parallel", "arbitrary"), vmem_mib),
        name=name,
    )(*args)
    return res


def _rope_table_body(pos_ref, inv_ref, t_ref):
    ang = pos_ref[...].astype(F32) * inv_ref[...]
    c = jnp.cos(ang)
    s = jnp.sin(ang)
    lane = lax.broadcasted_iota(I32, ang.shape, 1)
    t_ref[...] = jnp.where(lane < 2 * (QK_ROPE // 2), c, jnp.where(lane < 3 * (QK_ROPE // 2), -s, s))


def _rope_table(positions_col, tm=512):
    s = positions_col.shape[0]
    tm = min(tm, s)
    half = QK_ROPE // 2
    inv = ROPE_THETA ** (-jnp.arange(half, dtype=F32) / half)
    inv4 = jnp.tile(inv, 4).reshape(1, LANES)
    return pl.pallas_call(
        _rope_table_body,
        grid=(s // tm,),
        in_specs=[pl.BlockSpec((tm, 1), lambda i: (i, 0)), pl.BlockSpec((1, LANES), lambda i: (0, 0))],
        out_specs=pl.BlockSpec((tm, LANES), lambda i: (i, 0)),
        out_shape=jax.ShapeDtypeStruct((s, LANES), F32),
        compiler_params=_cparams(("parallel",), 16),
        name="rope_table",
    )(positions_col, inv4)


def _rope_128(v, table):
    t = v * table
    return t + pltpu.roll(t, 2 * (QK_ROPE // 2), axis=1)


def _softmax_step_t(st, vt, m_ref, acc_ref):
    m_prev = m_ref[...]
    m_new = jnp.maximum(m_prev, jnp.max(st, axis=0, keepdims=True))
    alpha = jnp.exp2(m_prev - m_new)
    p = jnp.exp2(st - m_new)
    acc_ref[...] = acc_ref[...] * alpha + jnp.dot(vt, p.astype(BF16), preferred_element_type=F32)
    m_ref[...] = m_new


def _causal_bias(tk, tq):
    key = jnp.arange(tq, dtype=I32).reshape(tq // tk, tk, 1)
    qry = jnp.arange(tq, dtype=I32).reshape(1, 1, tq)
    return jnp.where(key <= qry, 0.0, NEG).astype(F32)


def _init_stats(m_ref, acc_ref):
    m_ref[...] = jnp.full(m_ref.shape, NEG, F32)
    acc_ref[...] = jnp.zeros(acc_ref.shape, F32)


def _normalized(acc_ref, chain, dv):
    return acc_ref[chain, :dv, :] / acc_ref[chain, dv:dv + 1, :]


def _causal_pipeline(n_common, n_diag, chains, scores, update, s_ref, unroll=ATT_UNROLL):
    assert unroll % 2 == 0

    def ahead(c, slot):
        for a, st in zip(chains, scores(c)):
            s_ref[2 * a + slot] = st

    def fold(c, slot, d=None):
        for a in chains:
            update(c, a, s_ref[2 * a + slot], d)

    def step(c, t):
        ahead(c + 1, (t + 1) % 2)
        fold(c, t % 2)

    ahead(0, 0)
    n_groups = n_common // unroll

    def body(g, carry):
        for t in range(unroll):
            step(g * unroll + t, t)
        return carry

    lax.fori_loop(0, n_groups, body, 0)
    c0 = n_groups * unroll
    rem = n_common - c0
    for t in range(unroll - 1):
        @pl.when(t < rem)
        def _(t=t):
            step(c0 + t, t)

    for parity in range(2):
        @pl.when(rem % 2 == parity)
        def _(parity=parity):
            for d in range(n_diag):
                if d + 1 < n_diag:
                    ahead(n_common + d + 1, (parity + d + 1) % 2)
                fold(n_common + d, (parity + d) % 2, d)


def _mla_body(qt_ref, k_ref, vt_ref, mask_ref, o_ref, m_ref, acc_ref, s_ref, *, tq, tk):
    i = pl.program_id(1)
    _init_stats(m_ref, acc_ref)
    qt = qt_ref[...]

    def scores(c):
        return [jnp.dot(k_ref[pl.ds(pl.multiple_of(c * tk, tk), tk), :], qt, preferred_element_type=F32)]

    def update(c, a, st, d):
        if d is not None:
            st = st + mask_ref[d]
        _softmax_step_t(st, vt_ref[c], m_ref.at[a], acc_ref.at[a])

    _causal_pipeline(i * (tq // tk), tq // tk, [0], scores, update, s_ref)
    o_ref[...] = _normalized(acc_ref, 0, o_ref.shape[1]).T.astype(o_ref.dtype)


def _mla_attention(qt, k, vt):
    h, s, dk = k.shape
    _, nchunk, dvp, tk = vt.shape
    dv = dvp - ONES_ROWS
    tq = max(min(ATT_TQ, s), tk)
    return pl.pallas_call(
        functools.partial(_mla_body, tq=tq, tk=tk),
        grid=(h, s // tq),
        in_specs=[
            pl.BlockSpec((None, dk, tq), lambda hh, i: (hh, 0, i)),
            pl.BlockSpec((None, s, dk), lambda hh, i: (hh, 0, 0)),
            pl.BlockSpec((None, nchunk, dvp, tk), lambda hh, i: (hh, 0, 0, 0)),
            pl.BlockSpec((tq // tk, tk, tq), lambda hh, i: (0, 0, 0), pipeline_mode=pl.Buffered(1)),
        ],
        out_specs=pl.BlockSpec((tq, dv), lambda hh, i: (i, hh)),
        out_shape=jax.ShapeDtypeStruct((s, h * dv), BF16),
        scratch_shapes=[pltpu.VMEM((1, 1, tq), F32), pltpu.VMEM((1, dvp, tq), F32), pltpu.VMEM((2, tk, tq), F32)],
        compiler_params=_cparams(("parallel", "arbitrary"), 48),
        name="mla_attention",
    )(qt, k, vt, _causal_bias(tk, tq))


def _diff_body(slope_ref, qt_ref, k_ref, vt_ref, mask_ref, pcol_ref, prow_ref, lq1_ref, lk1_ref, lq2_ref, lk2_ref,
               sub_ref, o_ref, m_ref, acc_ref, s_ref, *, tq, tk, lam_init):
    h = pl.program_id(0)
    i = pl.program_id(1)
    _init_stats(m_ref, acc_ref)
    qts = [qt_ref[c * DH_B:(c + 1) * DH_B, :] for c in range(2)]
    pq = prow_ref[...]
    slope = slope_ref[h]

    def scores(c):
        start = pl.multiple_of(c * tk, tk)
        k = k_ref[pl.ds(start, tk), :]
        bias = jnp.abs(pcol_ref[pl.ds(start, tk), :] - pq) * slope
        return [jnp.dot(k[:, mp * DH_B:(mp + 1) * DH_B], qts[mp], preferred_element_type=F32) - bias
                for mp in range(2)]

    def update(c, chain, st, d):
        if d is not None:
            st = st + mask_ref[d]
        _softmax_step_t(st, vt_ref[c], m_ref.at[chain], acc_ref.at[chain])

    _causal_pipeline(i * (tq // tk), tq // tk, [0, 1], scores, update, s_ref)

    lam = (jnp.exp(jnp.sum(lq1_ref[...] * lk1_ref[...], axis=1, keepdims=True))
           - jnp.exp(jnp.sum(lq2_ref[...] * lk2_ref[...], axis=1, keepdims=True)) + lam_init)
    dv = o_ref.shape[1]
    o = _normalized(acc_ref, 0, dv) - lam * _normalized(acc_ref, 1, dv)
    ms = jnp.mean(o * o, axis=0, keepdims=True)
    y = o * lax.rsqrt(ms + EPS) * sub_ref[...] * (1.0 - lam_init)
    o_ref[...] = y.T.astype(o_ref.dtype)


def _diff_attention(qt, k, vt, pos_col, pos_row, slopes_l2, lam_vecs, subln, lam_init):
    _, s, dqk = k.shape
    _, nchunk, dvp, tk = vt.shape
    dv = dvp - ONES_ROWS
    tq = max(min(ATT_TQ, s), tk)
    vec = pl.BlockSpec((1, DH_B), lambda hh, i: (0, 0))
    once = pl.Buffered(1)
    return pl.pallas_call(
        functools.partial(_diff_body, tq=tq, tk=tk, lam_init=lam_init),
        grid=(HB, s // tq),
        in_specs=[
            pl.BlockSpec(memory_space=pltpu.SMEM),
            pl.BlockSpec((None, dqk, tq), lambda hh, i: (hh, 0, i)),
            pl.BlockSpec((None, s, dqk), lambda hh, i: (hh, 0, 0), pipeline_mode=once),
            pl.BlockSpec((None, nchunk, dvp, tk), lambda hh, i: (hh, 0, 0, 0), pipeline_mode=once),
            pl.BlockSpec((tq // tk, tk, tq), lambda hh, i: (0, 0, 0), pipeline_mode=once),
            pl.BlockSpec((s, 1), lambda hh, i: (0, 0), pipeline_mode=once),
            pl.BlockSpec((1, tq), lambda hh, i: (0, i)),
            vec, vec, vec, vec,
            pl.BlockSpec((dv, 1), lambda hh, i: (0, 0)),
        ],
        out_specs=pl.BlockSpec((tq, dv), lambda hh, i: (i, hh)),
        out_shape=jax.ShapeDtypeStruct((s, HB * dv), BF16),
        scratch_shapes=[pltpu.VMEM((2, 1, tq), F32), pltpu.VMEM((2, dvp, tq), F32), pltpu.VMEM((4, tk, tq), F32)],
        compiler_params=_cparams(("parallel", "arbitrary"), 56),
        name="diff_attention",
    )(slopes_l2, qt, k, vt, _causal_bias(tk, tq), pos_col, pos_row, *lam_vecs, subln.reshape(dv, 1))


def _merge_body(oa_ref, ob_ref, woa_ref, wob_ref, ga_ref, gb_ref, o_ref):
    ya = jnp.dot(oa_ref[...], woa_ref[...], preferred_element_type=F32)
    yb = jnp.dot(ob_ref[...], wob_ref[...], preferred_element_type=F32)
    out = jax.nn.sigmoid(ga_ref[...].astype(F32)) * ya + jax.nn.sigmoid(gb_ref[...].astype(F32)) * yb
    o_ref[...] = out.astype(o_ref.dtype)


def _merge(o_a, o_b, w_oa, w_ob, gates, tm=1024, tn=512):
    s, ka = o_a.shape
    kb = o_b.shape[1]
    d = w_oa.shape[1]
    tm = min(tm, s)
    nj = d // tn
    return pl.pallas_call(
        _merge_body,
        grid=(s // tm, nj),
        in_specs=[
            pl.BlockSpec((tm, ka), lambda i, j: (i, 0)),
            pl.BlockSpec((tm, kb), lambda i, j: (i, 0)),
            pl.BlockSpec((ka, tn), lambda i, j: (0, j)),
            pl.BlockSpec((kb, tn), lambda i, j: (0, j)),
            pl.BlockSpec((tm, tn), lambda i, j: (i, j)),
            pl.BlockSpec((tm, tn), lambda i, j: (i, nj + j)),
        ],
        out_specs=pl.BlockSpec((tm, tn), lambda i, j: (i, j)),
        out_shape=jax.ShapeDtypeStruct((s, d), BF16),
        compiler_params=_cparams(("parallel", "arbitrary"), 48),
        name="gated_merge",
    )(o_a, o_b, w_oa, w_ob, gates, gates)


def _pack_pairs(x):
    half = x.shape[1] // 2
    lo = pltpu.bitcast(x[:, :half].astype(BF16).astype(F32), U32)
    hi = pltpu.bitcast(x[:, half:].astype(BF16).astype(F32), U32)
    return (lo >> 16) | (hi & jnp.uint32(0xFFFF0000))


def _unpack_pairs(p):
    lo = pltpu.bitcast(p << 16, F32)
    hi = pltpu.bitcast(p & jnp.uint32(0xFFFF0000), F32)
    return jnp.concatenate([lo, hi], axis=1)


def _router_body(x_ref, g_ref, whi_ref, wlo_ref, b_ref, tri_ref, ids_ref, gates_ref, hp_ref, cnt_ref):
    @pl.when(pl.program_id(0) == 0)
    def _():
        cnt_ref[...] = jnp.zeros(cnt_ref.shape, F32)

    h = _rms(x_ref[...], g_ref[...])
    hp_ref[...] = _pack_pairs(h)
    h_hi = h.astype(BF16)
    h_lo = (h - h_hi.astype(F32)).astype(BF16)
    logits = (jnp.dot(h_hi, whi_ref[...], preferred_element_type=F32)
              + jnp.dot(h_lo, whi_ref[...], preferred_element_type=F32)
              + jnp.dot(h_hi, wlo_ref[...], preferred_element_type=F32)) + b_ref[...]
    lane = lax.broadcasted_iota(I32, logits.shape, 1).astype(F32)

    def first_argmax(vals, mx):
        return jnp.min(jnp.where(vals == mx, lane, 4.0 * LANES), axis=1, keepdims=True)

    is_g = jnp.logical_and(lane >= N_EXPERTS, lane < N_EXPERTS + N_GROUPS)
    lg = jnp.where(is_g, logits, NEG)
    mg = jnp.max(lg, axis=1, keepdims=True)
    p_g = 1.0 / jnp.sum(jnp.exp(lg - mg), axis=1, keepdims=True)
    g_idx = first_argmax(lg, mg) - N_EXPERTS
    lo = g_idx * EXP_PER_GROUP
    in_grp = jnp.logical_and(lane >= lo, lane < lo + EXP_PER_GROUP)
    le = jnp.where(in_grp, logits, NEG)
    m1 = jnp.max(le, axis=1, keepdims=True)
    i1 = first_argmax(le, m1)
    den = jnp.sum(jnp.exp(le - m1), axis=1, keepdims=True)
    le2 = jnp.where(lane == i1, NEG, le)
    m2 = jnp.max(le2, axis=1, keepdims=True)
    i2 = first_argmax(le2, m2)
    p1 = 1.0 / den
    p2 = jnp.exp(m2 - m1) / den
    psum = p1 + p2
    gate1 = p_g * p1 / psum
    gate2 = p_g * p2 / psum

    hot1 = jnp.where(lane == i1, 1.0, 0.0)
    hot2 = jnp.where(lane == i2, 1.0, 0.0)
    tri = tri_ref[...]
    before1 = jnp.dot(tri, hot1.astype(BF16), preferred_element_type=F32) + cnt_ref[...]
    tot1 = jnp.sum(hot1, axis=0, keepdims=True)
    before2 = jnp.dot(tri, hot2.astype(BF16), preferred_element_type=F32) + (cnt_ref[...] + tot1)
    r1 = jnp.sum(hot1 * before1, axis=1, keepdims=True)
    r2 = jnp.sum(hot2 * before2, axis=1, keepdims=True)
    cnt_ref[...] = cnt_ref[...] + tot1 + jnp.sum(hot2, axis=0, keepdims=True)

    ids = jnp.where(lane == 0, i1, jnp.where(lane == 1, i2, jnp.where(lane == 2, r1, jnp.where(lane == 3, r2, 0.0))))
    ids_ref[...] = ids.astype(I32)
    gates_ref[...] = jnp.where(lane == 0, gate1, jnp.where(lane == 1, gate2, 0.0))


def _router(x, gain, w_hi, w_lo, bias, tm=256):
    t, d = x.shape
    tm = min(tm, t)
    tri = jnp.tril(jnp.ones((tm, tm), F32), -1).astype(BF16)
    return pl.pallas_call(
        _router_body,
        grid=(t // tm,),
        in_specs=[
            pl.BlockSpec((tm, d), lambda i: (i, 0)),
            pl.BlockSpec((1, d), lambda i: (0, 0)),
            pl.BlockSpec((d, LANES), lambda i: (0, 0)),
            pl.BlockSpec((d, LANES), lambda i: (0, 0)),
            pl.BlockSpec((1, LANES), lambda i: (0, 0)),
            pl.BlockSpec((tm, tm), lambda i: (0, 0)),
        ],
        out_specs=[pl.BlockSpec((tm, LANES), lambda i: (i, 0)), pl.BlockSpec((tm, LANES), lambda i: (i, 0)),
                   pl.BlockSpec((tm, d // 2), lambda i: (i, 0)), pl.BlockSpec((1, LANES), lambda i: (0, 0))],
        out_shape=[jax.ShapeDtypeStruct((t, LANES), I32), jax.ShapeDtypeStruct((t, LANES), F32),
                   jax.ShapeDtypeStruct((t, d // 2), U32), jax.ShapeDtypeStruct((1, LANES), F32)],
        compiler_params=_cparams(("arbitrary",), 40),
        name="moe_router",
    )(x, gain.reshape(1, d), w_hi, w_lo, bias, tri)


def _dispatch_plan(ids, counts, t):
    rows = MOE_ROWS
    nb = t * TOP_K // rows + N_EXPERTS
    e = ids[:, :TOP_K]
    rank = ids[:, TOP_K:2 * TOP_K]
    cnt = counts[0, :N_EXPERTS].astype(I32)
    padded = (cnt + rows - 1) // rows * rows
    experts = jnp.arange(N_EXPERTS, dtype=I32)
    pad_end = jnp.sum(jnp.where(experts[None, :] <= experts[:, None], padded[None, :], 0), axis=1).astype(I32)
    pad_start = pad_end - padded
    hot = e[:, :, None] == experts
    dest = jnp.sum(jnp.where(hot, pad_start, 0), axis=-1).astype(I32) + rank
    nb_used = (pad_end[-1:] // rows).astype(I32)
    blk_row = jnp.arange(nb, dtype=I32) * rows
    blk_exp = jnp.minimum(jnp.sum(pad_end[None, :] <= blk_row[:, None], axis=1), N_EXPERTS - 1).astype(I32)
    last_blk_row = jnp.where(cnt > 0, pad_end - rows, -1).astype(I32)
    return dest, blk_exp, nb_used, last_blk_row


def _dispatch_body(last_ref, nbu_ref, dest_ref, hp_ref, xs_hbm, zbuf, sem_z, sem, *, tm, rows, nb):
    i = pl.program_id(0)

    @pl.when(i == 0)
    def _():
        zbuf[...] = jnp.zeros(zbuf.shape, U32)

        def expert_fill(e):
            return pltpu.make_async_copy(zbuf, xs_hbm.at[pl.ds(pl.multiple_of(last_ref[e], rows), rows)], sem_z)

        def tail_fill(b):
            return pltpu.make_async_copy(zbuf, xs_hbm.at[pl.ds(pl.multiple_of(b * rows, rows), rows)], sem_z)

        for wait in (False, True):
            def experts(e, c, wait=wait):
                @pl.when(last_ref[e] >= 0)
                def _():
                    expert_fill(e).wait() if wait else expert_fill(e).start()
                return c

            def tail(b, c, wait=wait):
                @pl.when(b >= nbu_ref[0])
                def _():
                    tail_fill(b).wait() if wait else tail_fill(b).start()
                return c

            lax.fori_loop(0, N_EXPERTS, experts, 0)
            lax.fori_loop(0, nb, tail, 0)

    def start_rows(r, c):
        for k in range(TOP_K):
            pltpu.make_async_copy(hp_ref.at[pl.ds(r, 1)], xs_hbm.at[pl.ds(dest_ref[0, TOP_K * r + k], 1)], sem).start()
        return c

    lax.fori_loop(0, tm, start_rows, 0, unroll=ROW_DMA_UNROLL)
    whole = xs_hbm.at[pl.ds(0, TOP_K * tm)]
    pltpu.make_async_copy(whole, whole, sem).wait()


def _dispatch(hp, dest, last_blk_row, nb_used, nb, tm=256):
    t, w = hp.shape
    tm = min(tm, t)
    rows = MOE_ROWS
    grid_spec = pltpu.PrefetchScalarGridSpec(
        num_scalar_prefetch=2,
        grid=(t // tm,),
        in_specs=[pl.BlockSpec((None, 1, TOP_K * tm), lambda i, lr, nu: (i, 0, 0), memory_space=pltpu.SMEM),
                  pl.BlockSpec((tm, w), lambda i, lr, nu: (i, 0))],
        out_specs=pl.BlockSpec(memory_space=pl.ANY),
        scratch_shapes=[pltpu.VMEM((rows, w), U32), pltpu.SemaphoreType.DMA(()), pltpu.SemaphoreType.DMA(())],
    )
    return pl.pallas_call(
        functools.partial(_dispatch_body, tm=tm, rows=rows, nb=nb),
        grid_spec=grid_spec,
        out_shape=jax.ShapeDtypeStruct((nb * rows, w), U32),
        compiler_params=_cparams(("arbitrary",), 16),
        name="moe_dispatch",
    )(last_blk_row, nb_used, dest.reshape(t // tm, 1, TOP_K * tm), hp)


def _expert_body(blk_exp_ref, nbu_ref, xs_ref, w1_ref, w3_ref, w2_ref, ys_ref):
    del blk_exp_ref
    b = pl.program_id(0)

    @pl.when(b < nbu_ref[0])
    def _():
        xn = _unpack_pairs(xs_ref[...]).astype(BF16)
        a1 = jnp.dot(xn, w1_ref[...], preferred_element_type=F32)
        a3 = jnp.dot(xn, w3_ref[...], preferred_element_type=F32)
        hmid = (jax.nn.silu(a1) * a3).astype(BF16)
        ys_ref[...] = _pack_pairs(jnp.dot(hmid, w2_ref[...], preferred_element_type=F32))

    @pl.when(b >= nbu_ref[0])
    def _():
        ys_ref[...] = jnp.zeros(ys_ref.shape, U32)


def _experts(xs, blk_exp, nb_used, w1, w3, w2):
    p, w = xs.shape
    _, d, de = w1.shape
    rows = MOE_ROWS
    grid_spec = pltpu.PrefetchScalarGridSpec(
        num_scalar_prefetch=2,
        grid=(p // rows,),
        in_specs=[
            pl.BlockSpec((rows, w), lambda b, be, nu: (jnp.minimum(b, nu[0] - 1), 0)),
            pl.BlockSpec((None, d, de), lambda b, be, nu: (be[b], 0, 0)),
            pl.BlockSpec((None, d, de), lambda b, be, nu: (be[b], 0, 0)),
            pl.BlockSpec((None, de, d), lambda b, be, nu: (be[b], 0, 0)),
        ],
        out_specs=pl.BlockSpec((rows, w), lambda b, be, nu: (b, 0)),
    )
    return pl.pallas_call(
        _expert_body,
        grid_spec=grid_spec,
        out_shape=jax.ShapeDtypeStruct((p, w), U32),
        compiler_params=_cparams(("arbitrary",), 40),
        name="moe_experts",
    )(blk_exp, nb_used, xs, w1, w3, w2)


def _combine_body(dest_ref, dest_next_ref, x_ref, gates_ref, g_ref, ys_hbm, *rest, tm, final):
    if final:
        o_ref, ybuf, sem = rest
    else:
        o_ref, h_ref, ybuf, sem = rest
    i = pl.program_id(0)
    slot = lax.rem(i, 2)

    def start_rows(d_ref, s):
        def rows_loop(r, c):
            for k in range(TOP_K):
                pltpu.make_async_copy(ys_hbm.at[pl.ds(d_ref[0, TOP_K * r + k], 1)], ybuf.at[s, k, pl.ds(r, 1)],
                                      sem.at[s]).start()
            return c

        lax.fori_loop(0, tm, rows_loop, 0, unroll=ROW_DMA_UNROLL)

    @pl.when(i == 0)
    def _():
        start_rows(dest_ref, 0)

    @pl.when(i + 1 < pl.num_programs(0))
    def _():
        start_rows(dest_next_ref, 1 - slot)

    pltpu.make_async_copy(ybuf.at[slot], ybuf.at[slot], sem.at[slot]).wait()
    gates = gates_ref[...]
    x = x_ref[...] + (gates[:, 0:1] * _unpack_pairs(ybuf[slot, 0]) + gates[:, 1:2] * _unpack_pairs(ybuf[slot, 1]))
    if final:
        o_ref[...] = _rms(x, g_ref[...])
    else:
        o_ref[...] = x
        h_ref[...] = _rms(x, g_ref[...]).astype(h_ref.dtype)


def _combine(x, ys, dest, gates, gain, final, tm=256):
    t, d = x.shape
    tm = min(tm, t)
    w = ys.shape[1]
    n = t // tm
    dest3 = dest.reshape(n, 1, TOP_K * tm)
    return pl.pallas_call(
        functools.partial(_combine_body, tm=tm, final=final),
        grid=(n,),
        in_specs=[pl.BlockSpec((None, 1, TOP_K * tm), lambda i: (i, 0, 0), memory_space=pltpu.SMEM),
                  pl.BlockSpec((None, 1, TOP_K * tm), lambda i: (jnp.minimum(i + 1, n - 1), 0, 0),
                               memory_space=pltpu.SMEM),
                  pl.BlockSpec((tm, d), lambda i: (i, 0)),
                  pl.BlockSpec((tm, LANES), lambda i: (i, 0)),
                  pl.BlockSpec((1, d), lambda i: (0, 0)),
                  pl.BlockSpec(memory_space=pl.ANY)],
        out_specs=[pl.BlockSpec((tm, d), lambda i: (i, 0))] * (1 if final else 2),
        out_shape=[jax.ShapeDtypeStruct((t, d), F32)] + ([] if final else [jax.ShapeDtypeStruct((t, d), BF16)]),
        scratch_shapes=[pltpu.VMEM((2, TOP_K, tm, w), U32), pltpu.SemaphoreType.DMA((2,))],
        compiler_params=_cparams(("arbitrary",), 48),
        name="moe_combine",
    )(dest3, dest3, x, gates, gain.reshape(1, d), ys)


def _split_w_in_body(wt_ref, lat_ref, qb_ref, kb_ref, vb_ref, gate_ref):
    wt = wt_ref[...]
    tr = wt.shape[1]
    d = gate_ref.shape[1] // 2
    o = np.cumsum([0, Q_LORA, KV_LORA, QK_ROPE, HB * 2 * DH_B, HB * 2 * DH_B, HB * 2 * DH_B, 2 * d])
    half = QK_ROPE // 2
    x1, x2 = wt[o[2]:o[2] + half], wt[o[2] + half:o[3]]
    pad = jnp.zeros((lat_ref.shape[1] - Q_LORA - KV_LORA - LANES, tr), wt.dtype)
    lat_ref[...] = jnp.concatenate([wt[:o[2]], x1, x2, x2, x1, pad], axis=0).T.astype(BF16)
    qb_ref[...] = wt[o[3]:o[4]].T.astype(BF16)
    kb_ref[...] = wt[o[4]:o[5]].T.astype(BF16)
    vb_ref[...] = wt[o[5]:o[6]].T.astype(BF16)
    gate_ref[...] = wt[o[6]:o[7]].T.astype(BF16)


def _split_w_in(w_in, tr=128):
    layers, d, d_in = w_in.shape
    wb = HB * 2 * DH_B
    widths = (2048, wb, wb, wb, 2 * d)
    return pl.pallas_call(
        _split_w_in_body,
        grid=(layers, d // tr),
        in_specs=[pl.BlockSpec((None, d_in, tr), lambda l, i: (l, 0, i))],
        out_specs=[pl.BlockSpec((None, tr, wd), lambda l, i: (l, i, 0)) for wd in widths],
        out_shape=[jax.ShapeDtypeStruct((layers, d, wd), BF16) for wd in widths],
        compiler_params=_cparams(("parallel", "parallel"), 48),
        name="split_w_in",
    )(jnp.swapaxes(w_in, 1, 2))


def _rope_cols(w_rope):
    half = QK_ROPE // 2
    x1, x2 = w_rope[..., :half], w_rope[..., half:]
    return jnp.concatenate([x1, x2, x2, x1], axis=-1)


def _layer_weights(w_uq, w_ukv, w_router_g, b_router_g, w_router_e, b_router_e):
    d = w_router_g.shape[0]
    wq = w_uq.reshape(Q_LORA, HA, QK_NOPE + QK_ROPE)
    wq = jnp.concatenate([wq[..., :QK_NOPE], _rope_cols(wq[..., QK_NOPE:])], axis=-1).reshape(Q_LORA, HA * 2 * LANES)
    w_r = jnp.concatenate([w_router_e, w_router_g, jnp.zeros((d, LANES - N_EXPERTS - N_GROUPS), F32)], axis=1)
    w_r_hi = w_r.astype(BF16)
    w_r_lo = (w_r - w_r_hi.astype(F32)).astype(BF16)
    b_r = jnp.concatenate([b_router_e, b_router_g, jnp.zeros((LANES - N_EXPERTS - N_GROUPS,), F32)]).reshape(1, LANES)
    return wq.astype(BF16), w_ukv.astype(BF16), w_r_hi, w_r_lo, b_r


def kernel(x, positions, norm_attn, w_in, q_norm, w_uq, kv_norm, w_ukv, lam_q1, lam_k1, lam_q2, lam_k2, subln, w_oa,
           w_ob, w_out, norm_ffn, w_router_g, b_router_g, w_router_e, b_router_e, w1, w3, w2, norm_final):
    bsz, s, d = x.shape
    assert bsz == 1
    depth = w_in.shape[0]
    x = x.reshape(s, d)
    pos_col = positions.reshape(s, 1)
    table = _rope_table(pos_col)
    pos_col_f = pos_col.astype(F32)
    pos_row_f = pos_col_f.reshape(1, s)
    slopes_l2 = jnp.asarray(2.0 ** (-8.0 * (np.arange(HB) + 1) / HB) * LOG2E, dtype=F32)
    c_a = (QK_NOPE + QK_ROPE) ** -0.5 * LOG2E
    c_b = DH_B ** -0.5 * LOG2E
    tm = min(1024, s)
    tk_a, tk_b = min(ATT_TK_A, s), min(ATT_TK_B, s)
    hd = 2 * LANES

    def chunked_transpose(acc, tk):
        t = jnp.concatenate([acc.T, jnp.ones((ONES_ROWS, tm), F32)], axis=0)
        return jnp.stack([t[:, c * tk:(c + 1) * tk] for c in range(tm // tk)])

    w_lat, w_qb, w_kb, w_vb, w_gate = _split_w_in(w_in)
    n_exp = w1.shape[1]
    w1b, w3b, w2b = (w.reshape((depth * n_exp,) + w.shape[2:]).astype(BF16) for w in (w1, w3, w2))

    for l in range(depth):
        wq, wkv, w_r_hi, w_r_lo, b_r = _layer_weights(
            w_uq[l], w_ukv[l], w_router_g[l], b_router_g[l], w_router_e[l], b_router_e[l])
        if l == 0:
            h = _rmsnorm(x, norm_attn[l], BF16)
        full = ((tm, d), lambda i, j: (i, 0))

        def ident(acc):
            return (acc,)

        (lat,) = _matmul("in_proj_latent", h, *full, w_lat, 512,
                         [(jax.ShapeDtypeStruct((s, 2048), BF16), (tm, 512), lambda i, j: (i, j))], ident, b_layer=l)

        (qt_b,) = _matmul("in_proj_q_b", h, *full, w_qb, hd,
                          [(jax.ShapeDtypeStruct((HB, hd, s), BF16), (None, hd, tm), lambda i, j: (j, 0, i))],
                          lambda acc: ((acc * c_b).T,), b_layer=l)
        (k_b,) = _matmul("in_proj_k_b", h, *full, w_kb, hd,
                         [(jax.ShapeDtypeStruct((HB, s, hd), BF16), (None, tm, hd), lambda i, j: (j, i, 0))], ident,
                         b_layer=l)
        (vt_b,) = _matmul("in_proj_v_b", h, *full, w_vb, hd,
                          [(jax.ShapeDtypeStruct((HB, s // tk_b, hd + ONES_ROWS, tk_b), BF16),
                            (None, tm // tk_b, hd + ONES_ROWS, tk_b),
                            lambda i, j: (j, i, 0, 0))], lambda acc: (chunked_transpose(acc, tk_b),), b_layer=l)
        (gates,) = _matmul("in_proj_gates", h, *full, w_gate, 512,
                           [(jax.ShapeDtypeStruct((s, 2 * d), BF16), (tm, 512), lambda i, j: (i, j))], ident,
                           b_layer=l)

        def q_epilogue(acc, t_ref):
            heads = []
            for hh in range(2):
                a = acc[:, hh * hd:(hh + 1) * hd]
                rot = _rope_128(a[:, LANES:], t_ref[...])
                heads.append((jnp.concatenate([a[:, :LANES], rot], axis=1) * c_a).T)
            return (jnp.stack(heads),)

        (qt_a,) = _matmul("q_up_proj", lat, (tm, Q_LORA), lambda i, j: (i, 0), wq, 2 * hd,
                          [(jax.ShapeDtypeStruct((HA, hd, s), BF16), (2, hd, tm), lambda i, j: (j, 0, i))],
                          q_epilogue, gain=q_norm[l], extras=[(table, (tm, LANES), lambda i, j: (i, 0))])

        def kv_epilogue(acc, kr_ref, t_ref):
            rot = _rope_128(kr_ref[...].astype(F32), t_ref[...])
            lane = lax.broadcasted_iota(I32, rot.shape, 1)
            rot = jnp.where(lane < QK_ROPE, rot, 0.0)
            return jnp.concatenate([acc[:, :LANES], rot], axis=1), chunked_transpose(acc[:, LANES:], tk_a)

        k_a, vt_a = _matmul("kv_up_proj", lat, (tm, KV_LORA), lambda i, j: (i, Q_LORA // KV_LORA), wkv, hd,
                            [(jax.ShapeDtypeStruct((HA, s, hd), BF16), (None, tm, hd), lambda i, j: (j, i, 0)),
                             (jax.ShapeDtypeStruct((HA, s // tk_a, V_DIM_A + ONES_ROWS, tk_a), BF16),
                              (None, tm // tk_a, V_DIM_A + ONES_ROWS, tk_a), lambda i, j: (j, i, 0, 0))],
                            kv_epilogue, gain=kv_norm[l],
                            extras=[(lat, (tm, LANES), lambda i, j: (i, (Q_LORA + KV_LORA) // LANES)),
                                    (table, (tm, LANES), lambda i, j: (i, 0))])
        o_a = _mla_attention(qt_a, k_a, vt_a)

        lam_init = 0.8 - 0.6 * math.exp(-0.3 * l)
        lam_vecs = [v[l].reshape(1, DH_B).astype(F32) for v in (lam_q1, lam_k1, lam_q2, lam_k2)]
        o_b = _diff_attention(qt_b, k_b, vt_b, pos_col_f, pos_row_f, slopes_l2, lam_vecs, subln[l], lam_init)

        merged = _merge(o_a, o_b, w_oa[l].astype(BF16), w_ob[l].astype(BF16), gates)

        def residual(acc, x_ref):
            return (x_ref[...] + acc,)

        (x,) = _matmul("out_proj", merged, (tm, d), lambda i, j: (i, 0), w_out[l].astype(BF16), 512,
                       [(jax.ShapeDtypeStruct((s, d), F32), (tm, 512), lambda i, j: (i, j))], residual,
                       extras=[(x, (tm, 512), lambda i, j: (i, j))])

        ids, gate_vals, hp, counts = _router(x, norm_ffn[l], w_r_hi, w_r_lo, b_r)
        dest, blk_exp, nb_used, last_blk_row = _dispatch_plan(ids, counts, s)
        xs = _dispatch(hp, dest, last_blk_row, nb_used, blk_exp.shape[0])
        ys = _experts(xs, blk_exp + l * n_exp, nb_used, w1b, w3b, w2b)
        final = l == depth - 1
        x, *h_next = _combine(x, ys, dest, gate_vals, norm_final if final else norm_attn[l + 1], final)
        h = h_next[0] if h_next else None
    return x.reshape(bsz, s, d)
```

```python
import functools
import math

import jax
import jax.numpy as jnp
import numpy as np
from jax import lax
from jax.experimental import pallas as pl
from jax.experimental.pallas import tpu as pltpu

F32 = jnp.float32
BF16 = jnp.bfloat16
I32 = jnp.int32
U32 = jnp.uint32

EPS = 1e-6
ROPE_THETA = 10000.0
HA, QK_NOPE, QK_ROPE, V_DIM_A = 16, 128, 64, 128
Q_LORA, KV_LORA = 1024, 512
HB, DH_B = 8, 128
N_GROUPS, EXP_PER_GROUP, TOP_K = 8, 8, 2
N_EXPERTS = N_GROUPS * EXP_PER_GROUP
LOG2E = 1.4426950408889634
NEG = -1e30

LANES = 128
MOE_ROWS = 128
ATT_TK_A = 1024
ATT_TK_B = 512
ATT_TQ = 1024
ATT_UNROLL = 2
ROW_DMA_UNROLL = 8
ONES_ROWS = 16
MIB = 1024 * 1024


def _cparams(semantics, vmem_mib):
    return pltpu.CompilerParams(dimension_semantics=semantics, vmem_limit_bytes=vmem_mib * MIB)


def _rms(xf, gain):
    ms = jnp.mean(xf * xf, axis=-1, keepdims=True)
    return xf * lax.rsqrt(ms + EPS) * gain


def _rmsnorm_body(x_ref, g_ref, o_ref):
    o_ref[...] = _rms(x_ref[...], g_ref[...]).astype(o_ref.dtype)


def _rmsnorm(x, gain, out_dtype, tm=256):
    s, d = x.shape
    tm = min(tm, s)
    return pl.pallas_call(
        _rmsnorm_body,
        grid=(s // tm,),
        in_specs=[pl.BlockSpec((tm, d), lambda i: (i, 0)), pl.BlockSpec((1, d), lambda i: (0, 0))],
        out_specs=pl.BlockSpec((tm, d), lambda i: (i, 0)),
        out_shape=jax.ShapeDtypeStruct((s, d), out_dtype),
        compiler_params=_cparams(("parallel",), 32),
        name="rmsnorm",
    )(x, gain.reshape(1, d))


def _matmul_body(*refs, n_extra, n_out, norm, epilogue):
    a_ref = refs[0]
    pos = 1
    if norm:
        g_ref = refs[pos]
        pos += 1
    b_ref = refs[pos]
    pos += 1
    extras = refs[pos:pos + n_extra]
    pos += n_extra
    outs = refs[pos:pos + n_out]
    pos += n_out
    if norm:
        lhs_ref = refs[pos]

        @pl.when(pl.program_id(1) == 0)
        def _():
            lhs_ref[...] = _rms(a_ref[...].astype(F32), g_ref[...]).astype(BF16)

        lhs = lhs_ref[...]
    else:
        lhs = a_ref[...]
    acc = jnp.dot(lhs, b_ref[...], preferred_element_type=F32)
    for o_ref, r in zip(outs, epilogue(acc, *extras)):
        o_ref[...] = r.astype(o_ref.dtype)


def _matmul(name, a, a_block, a_index, b, tn, outs, epilogue, gain=None, extras=(), vmem_mib=48, b_layer=None):
    tm, k = a_block
    m = a.shape[0]
    n = b.shape[-1]
    norm = gain is not None
    in_specs = [pl.BlockSpec(a_block, a_index)]
    args = [a]
    if norm:
        in_specs.append(pl.BlockSpec((1, k), lambda i, j: (0, 0)))
        args.append(gain.reshape(1, k).astype(F32))
    if b_layer is None:
        in_specs.append(pl.BlockSpec((k, tn), lambda i, j: (0, j)))
    else:
        in_specs.append(pl.BlockSpec((None, k, tn), lambda i, j: (b_layer, 0, j)))
    args.append(b)
    for arr, blk, imap in extras:
        in_specs.append(pl.BlockSpec(blk, imap))
        args.append(arr)
    res = pl.pallas_call(
        functools.partial(_matmul_body, n_extra=len(extras), n_out=len(outs), norm=norm, epilogue=epilogue),
        grid=(m // tm, n // tn),
        in_specs=in_specs,
        out_specs=[pl.BlockSpec(blk, imap) for _, blk, imap in outs],
        out_shape=[sds for sds, _, _ in outs],
        scratch_shapes=[pltpu.VMEM((tm, k), BF16)] if norm else [],
        compiler_params=_cparams(("parallel", "arbitrary"), vmem_mib),
        name=name,
    )(*args)
    return res


def _rope_table_body(pos_ref, inv_ref, t_ref):
    ang = pos_ref[...].astype(F32) * inv_ref[...]
    c = jnp.cos(ang)
    s = jnp.sin(ang)
    lane = lax.broadcasted_iota(I32, ang.shape, 1)
    t_ref[...] = jnp.where(lane < 2 * (QK_ROPE // 2), c, jnp.where(lane < 3 * (QK_ROPE // 2), -s, s))


def _rope_table(positions_col, tm=512):
    s = positions_col.shape[0]
    tm = min(tm, s)
    half = QK_ROPE // 2
    inv = ROPE_THETA ** (-jnp.arange(half, dtype=F32) / half)
    inv4 = jnp.tile(inv, 4).reshape(1, LANES)
    return pl.pallas_call(
        _rope_table_body,
        grid=(s // tm,),
        in_specs=[pl.BlockSpec((tm, 1), lambda i: (i, 0)), pl.BlockSpec((1, LANES), lambda i: (0, 0))],
        out_specs=pl.BlockSpec((tm, LANES), lambda i: (i, 0)),
        out_shape=jax.ShapeDtypeStruct((s, LANES), F32),
        compiler_params=_cparams(("parallel",), 16),
        name="rope_table",
    )(positions_col, inv4)


def _rope_128(v, table):
    t = v * table
    return t + pltpu.roll(t, 2 * (QK_ROPE // 2), axis=1)


def _softmax_step_t(st, vt, m_ref, acc_ref):
    m_prev = m_ref[...]
    m_new = jnp.maximum(m_prev, jnp.max(st, axis=0, keepdims=True))
    alpha = jnp.exp2(m_prev - m_new)
    p = jnp.exp2(st - m_new)
    acc_ref[...] = acc_ref[...] * alpha + jnp.dot(vt, p.astype(BF16), preferred_element_type=F32)
    m_ref[...] = m_new


def _causal_bias(tk, tq):
    key = jnp.arange(tq, dtype=I32).reshape(tq // tk, tk, 1)
    qry = jnp.arange(tq, dtype=I32).reshape(1, 1, tq)
    return jnp.where(key <= qry, 0.0, NEG).astype(F32)


def _init_stats(m_ref, acc_ref):
    m_ref[...] = jnp.full(m_ref.shape, NEG, F32)
    acc_ref[...] = jnp.zeros(acc_ref.shape, F32)


def _normalized(acc_ref, chain, dv):
    return acc_ref[chain, :dv, :] / acc_ref[chain, dv:dv + 1, :]


def _causal_pipeline(n_common, n_diag, chains, scores, update, s_ref, unroll=ATT_UNROLL):
    assert unroll % 2 == 0

    def ahead(c, slot):
        for a, st in zip(chains, scores(c)):
            s_ref[2 * a + slot] = st

    def fold(c, slot, d=None):
        for a in chains:
            update(c, a, s_ref[2 * a + slot], d)

    def step(c, t):
        ahead(c + 1, (t + 1) % 2)
        fold(c, t % 2)

    ahead(0, 0)
    n_groups = n_common // unroll

    def body(g, carry):
        for t in range(unroll):
            step(g * unroll + t, t)
        return carry

    lax.fori_loop(0, n_groups, body, 0)
    c0 = n_groups * unroll
    rem = n_common - c0
    for t in range(unroll - 1):
        @pl.when(t < rem)
        def _(t=t):
            step(c0 + t, t)

    for parity in range(2):
        @pl.when(rem % 2 == parity)
        def _(parity=parity):
            for d in range(n_diag):
                if d + 1 < n_diag:
                    ahead(n_common + d + 1, (parity + d + 1) % 2)
                fold(n_common + d, (parity + d) % 2, d)


def _mla_body(qt_ref, k_ref, vt_ref, mask_ref, o_ref, m_ref, acc_ref, s_ref, *, tq, tk):
    i = pl.program_id(1)
    _init_stats(m_ref, acc_ref)
    qt = qt_ref[...]

    def scores(c):
        return [jnp.dot(k_ref[pl.ds(pl.multiple_of(c * tk, tk), tk), :], qt, preferred_element_type=F32)]

    def update(c, a, st, d):
        if d is not None:
            st = st + mask_ref[d]
        _softmax_step_t(st, vt_ref[c], m_ref.at[a], acc_ref.at[a])

    _causal_pipeline(i * (tq // tk), tq // tk, [0], scores, update, s_ref)
    o_ref[...] = _normalized(acc_ref, 0, o_ref.shape[1]).T.astype(o_ref.dtype)


def _mla_attention(qt, k, vt):
    h, s, dk = k.shape
    _, nchunk, dvp, tk = vt.shape
    dv = dvp - ONES_ROWS
    tq = max(min(ATT_TQ, s), tk)
    return pl.pallas_call(
        functools.partial(_mla_body, tq=tq, tk=tk),
        grid=(h, s // tq),
        in_specs=[
            pl.BlockSpec((None, dk, tq), lambda hh, i: (hh, 0, i)),
            pl.BlockSpec((None, s, dk), lambda hh, i: (hh, 0, 0)),
            pl.BlockSpec((None, nchunk, dvp, tk), lambda hh, i: (hh, 0, 0, 0)),
            pl.BlockSpec((tq // tk, tk, tq), lambda hh, i: (0, 0, 0), pipeline_mode=pl.Buffered(1)),
        ],
        out_specs=pl.BlockSpec((tq, dv), lambda hh, i: (i, hh)),
        out_shape=jax.ShapeDtypeStruct((s, h * dv), BF16),
        scratch_shapes=[pltpu.VMEM((1, 1, tq), F32), pltpu.VMEM((1, dvp, tq), F32), pltpu.VMEM((2, tk, tq), F32)],
        compiler_params=_cparams(("parallel", "arbitrary"), 48),
        name="mla_attention",
    )(qt, k, vt, _causal_bias(tk, tq))


def _diff_body(slope_ref, qt_ref, k_ref, vt_ref, mask_ref, pcol_ref, prow_ref, lq1_ref, lk1_ref, lq2_ref, lk2_ref,
               sub_ref, o_ref, m_ref, acc_ref, s_ref, *, tq, tk, lam_init):
    h = pl.program_id(0)
    i = pl.program_id(1)
    _init_stats(m_ref, acc_ref)
    qts = [qt_ref[c * DH_B:(c + 1) * DH_B, :] for c in range(2)]
    pq = prow_ref[...]
    slope = slope_ref[h]

    def scores(c):
        start = pl.multiple_of(c * tk, tk)
        k = k_ref[pl.ds(start, tk), :]
        bias = jnp.abs(pcol_ref[pl.ds(start, tk), :] - pq) * slope
        return [jnp.dot(k[:, mp * DH_B:(mp + 1) * DH_B], qts[mp], preferred_element_type=F32) - bias
                for mp in range(2)]

    def update(c, chain, st, d):
        if d is not None:
            st = st + mask_ref[d]
        _softmax_step_t(st, vt_ref[c], m_ref.at[chain], acc_ref.at[chain])

    _causal_pipeline(i * (tq // tk), tq // tk, [0, 1], scores, update, s_ref)

    lam = (jnp.exp(jnp.sum(lq1_ref[...] * lk1_ref[...], axis=1, keepdims=True))
           - jnp.exp(jnp.sum(lq2_ref[...] * lk2_ref[...], axis=1, keepdims=True)) + lam_init)
    dv = o_ref.shape[1]
    o = _normalized(acc_ref, 0, dv) - lam * _normalized(acc_ref, 1, dv)
    ms = jnp.mean(o * o, axis=0, keepdims=True)
    y = o * lax.rsqrt(ms + EPS) * sub_ref[...] * (1.0 - lam_init)
    o_ref[...] = y.T.astype(o_ref.dtype)


def _diff_attention(qt, k, vt, pos_col, pos_row, slopes_l2, lam_vecs, subln, lam_init):
    _, s, dqk = k.shape
    _, nchunk, dvp, tk = vt.shape
    dv = dvp - ONES_ROWS
    tq = max(min(ATT_TQ, s), tk)
    vec = pl.BlockSpec((1, DH_B), lambda hh, i: (0, 0))
    once = pl.Buffered(1)
    return pl.pallas_call(
        functools.partial(_diff_body, tq=tq, tk=tk, lam_init=lam_init),
        grid=(HB, s // tq),
        in_specs=[
            pl.BlockSpec(memory_space=pltpu.SMEM),
            pl.BlockSpec((None, dqk, tq), lambda hh, i: (hh, 0, i)),
            pl.BlockSpec((None, s, dqk), lambda hh, i: (hh, 0, 0), pipeline_mode=once),
            pl.BlockSpec((None, nchunk, dvp, tk), lambda hh, i: (hh, 0, 0, 0), pipeline_mode=once),
            pl.BlockSpec((tq // tk, tk, tq), lambda hh, i: (0, 0, 0), pipeline_mode=once),
            pl.BlockSpec((s, 1), lambda hh, i: (0, 0), pipeline_mode=once),
            pl.BlockSpec((1, tq), lambda hh, i: (0, i)),
            vec, vec, vec, vec,
            pl.BlockSpec((dv, 1), lambda hh, i: (0, 0)),
        ],
        out_specs=pl.BlockSpec((tq, dv), lambda hh, i: (i, hh)),
        out_shape=jax.ShapeDtypeStruct((s, HB * dv), BF16),
        scratch_shapes=[pltpu.VMEM((2, 1, tq), F32), pltpu.VMEM((2, dvp, tq), F32), pltpu.VMEM((4, tk, tq), F32)],
        compiler_params=_cparams(("parallel", "arbitrary"), 56),
        name="diff_attention",
    )(slopes_l2, qt, k, vt, _causal_bias(tk, tq), pos_col, pos_row, *lam_vecs, subln.reshape(dv, 1))


def _merge_body(oa_ref, ob_ref, woa_ref, wob_ref, ga_ref, gb_ref, o_ref):
    ya = jnp.dot(oa_ref[...], woa_ref[...], preferred_element_type=F32)
    yb = jnp.dot(ob_ref[...], wob_ref[...], preferred_element_type=F32)
    out = jax.nn.sigmoid(ga_ref[...].astype(F32)) * ya + jax.nn.sigmoid(gb_ref[...].astype(F32)) * yb
    o_ref[...] = out.astype(o_ref.dtype)


def _merge(o_a, o_b, w_oa, w_ob, gates, tm=1024, tn=512):
    s, ka = o_a.shape
    kb = o_b.shape[1]
    d = w_oa.shape[1]
    tm = min(tm, s)
    nj = d // tn
    return pl.pallas_call(
        _merge_body,
        grid=(s // tm, nj),
        in_specs=[
            pl.BlockSpec((tm, ka), lambda i, j: (i, 0)),
            pl.BlockSpec((tm, kb), lambda i, j: (i, 0)),
            pl.BlockSpec((ka, tn), lambda i, j: (0, j)),
            pl.BlockSpec((kb, tn), lambda i, j: (0, j)),
            pl.BlockSpec((tm, tn), lambda i, j: (i, j)),
            pl.BlockSpec((tm, tn), lambda i, j: (i, nj + j)),
        ],
        out_specs=pl.BlockSpec((tm, tn), lambda i, j: (i, j)),
        out_shape=jax.ShapeDtypeStruct((s, d), BF16),
        compiler_params=_cparams(("parallel", "arbitrary"), 48),
        name="gated_merge",
    )(o_a, o_b, w_oa, w_ob, gates, gates)


def _pack_pairs(x):
    half = x.shape[1] // 2
    lo = pltpu.bitcast(x[:, :half].astype(BF16).astype(F32), U32)
    hi = pltpu.bitcast(x[:, half:].astype(BF16).astype(F32), U32)
    return (lo >> 16) | (hi & jnp.uint32(0xFFFF0000))


def _unpack_pairs(p):
    lo = pltpu.bitcast(p << 16, F32)
    hi = pltpu.bitcast(p & jnp.uint32(0xFFFF0000), F32)
    return jnp.concatenate([lo, hi], axis=1)


def _router_body(x_ref, g_ref, whi_ref, wlo_ref, b_ref, tri_ref, ids_ref, gates_ref, hp_ref, cnt_ref):
    @pl.when(pl.program_id(0) == 0)
    def _():
        cnt_ref[...] = jnp.zeros(cnt_ref.shape, F32)

    h = _rms(x_ref[...], g_ref[...])
    hp_ref[...] = _pack_pairs(h)
    h_hi = h.astype(BF16)
    h_lo = (h - h_hi.astype(F32)).astype(BF16)
    logits = (jnp.dot(h_hi, whi_ref[...], preferred_element_type=F32)
              + jnp.dot(h_lo, whi_ref[...], preferred_element_type=F32)
              + jnp.dot(h_hi, wlo_ref[...], preferred_element_type=F32)) + b_ref[...]
    lane = lax.broadcasted_iota(I32, logits.shape, 1).astype(F32)

    def first_argmax(vals, mx):
        return jnp.min(jnp.where(vals == mx, lane, 4.0 * LANES), axis=1, keepdims=True)

    is_g = jnp.logical_and(lane >= N_EXPERTS, lane < N_EXPERTS + N_GROUPS)
    lg = jnp.where(is_g, logits, NEG)
    mg = jnp.max(lg, axis=1, keepdims=True)
    p_g = 1.0 / jnp.sum(jnp.exp(lg - mg), axis=1, keepdims=True)
    g_idx = first_argmax(lg, mg) - N_EXPERTS
    lo = g_idx * EXP_PER_GROUP
    in_grp = jnp.logical_and(lane >= lo, lane < lo + EXP_PER_GROUP)
    le = jnp.where(in_grp, logits, NEG)
    m1 = jnp.max(le, axis=1, keepdims=True)
    i1 = first_argmax(le, m1)
    den = jnp.sum(jnp.exp(le - m1), axis=1, keepdims=True)
    le2 = jnp.where(lane == i1, NEG, le)
    m2 = jnp.max(le2, axis=1, keepdims=True)
    i2 = first_argmax(le2, m2)
    p1 = 1.0 / den
    p2 = jnp.exp(m2 - m1) / den
    psum = p1 + p2
    gate1 = p_g * p1 / psum
    gate2 = p_g * p2 / psum

    hot1 = jnp.where(lane == i1, 1.0, 0.0)
    hot2 = jnp.where(lane == i2, 1.0, 0.0)
    tri = tri_ref[...]
    before1 = jnp.dot(tri, hot1.astype(BF16), preferred_element_type=F32) + cnt_ref[...]
    tot1 = jnp.sum(hot1, axis=0, keepdims=True)
    before2 = jnp.dot(tri, hot2.astype(BF16), preferred_element_type=F32) + (cnt_ref[...] + tot1)
    r1 = jnp.sum(hot1 * before1, axis=1, keepdims=True)
    r2 = jnp.sum(hot2 * before2, axis=1, keepdims=True)
    cnt_ref[...] = cnt_ref[...] + tot1 + jnp.sum(hot2, axis=0, keepdims=True)

    ids = jnp.where(lane == 0, i1, jnp.where(lane == 1, i2, jnp.where(lane == 2, r1, jnp.where(lane == 3, r2, 0.0))))
    ids_ref[...] = ids.astype(I32)
    gates_ref[...] = jnp.where(lane == 0, gate1, jnp.where(lane == 1, gate2, 0.0))


def _router(x, gain, w_hi, w_lo, bias, tm=256):
    t, d = x.shape
    tm = min(tm, t)
    tri = jnp.tril(jnp.ones((tm, tm), F32), -1).astype(BF16)
    return pl.pallas_call(
        _router_body,
        grid=(t // tm,),
        in_specs=[
            pl.BlockSpec((tm, d), lambda i: (i, 0)),
            pl.BlockSpec((1, d), lambda i: (0, 0)),
            pl.BlockSpec((d, LANES), lambda i: (0, 0)),
            pl.BlockSpec((d, LANES), lambda i: (0, 0)),
            pl.BlockSpec((1, LANES), lambda i: (0, 0)),
            pl.BlockSpec((tm, tm), lambda i: (0, 0)),
        ],
        out_specs=[pl.BlockSpec((tm, LANES), lambda i: (i, 0)), pl.BlockSpec((tm, LANES), lambda i: (i, 0)),
                   pl.BlockSpec((tm, d // 2), lambda i: (i, 0)), pl.BlockSpec((1, LANES), lambda i: (0, 0))],
        out_shape=[jax.ShapeDtypeStruct((t, LANES), I32), jax.ShapeDtypeStruct((t, LANES), F32),
                   jax.ShapeDtypeStruct((t, d // 2), U32), jax.ShapeDtypeStruct((1, LANES), F32)],
        compiler_params=_cparams(("arbitrary",), 40),
        name="moe_router",
    )(x, gain.reshape(1, d), w_hi, w_lo, bias, tri)


def _dispatch_plan(ids, counts, t):
    rows = MOE_ROWS
    nb = t * TOP_K // rows + N_EXPERTS
    e = ids[:, :TOP_K]
    rank = ids[:, TOP_K:2 * TOP_K]
    cnt = counts[0, :N_EXPERTS].astype(I32)
    padded = (cnt + rows - 1) // rows * rows
    experts = jnp.arange(N_EXPERTS, dtype=I32)
    pad_end = jnp.sum(jnp.where(experts[None, :] <= experts[:, None], padded[None, :], 0), axis=1).astype(I32)
    pad_start = pad_end - padded
    hot = e[:, :, None] == experts
    dest = jnp.sum(jnp.where(hot, pad_start, 0), axis=-1).astype(I32) + rank
    nb_used = (pad_end[-1:] // rows).astype(I32)
    blk_row = jnp.arange(nb, dtype=I32) * rows
    blk_exp = jnp.minimum(jnp.sum(pad_end[None, :] <= blk_row[:, None], axis=1), N_EXPERTS - 1).astype(I32)
    last_blk_row = jnp.where(cnt > 0, pad_end - rows, -1).astype(I32)
    return dest, blk_exp, nb_used, last_blk_row


def _dispatch_body(last_ref, nbu_ref, dest_ref, hp_ref, xs_hbm, zbuf, sem_z, sem, *, tm, rows, nb):
    i = pl.program_id(0)

    @pl.when(i == 0)
    def _():
        zbuf[...] = jnp.zeros(zbuf.shape, U32)

        def expert_fill(e):
            return pltpu.make_async_copy(zbuf, xs_hbm.at[pl.ds(pl.multiple_of(last_ref[e], rows), rows)], sem_z)

        def tail_fill(b):
            return pltpu.make_async_copy(zbuf, xs_hbm.at[pl.ds(pl.multiple_of(b * rows, rows), rows)], sem_z)

        for wait in (False, True):
            def experts(e, c, wait=wait):
                @pl.when(last_ref[e] >= 0)
                def _():
                    expert_fill(e).wait() if wait else expert_fill(e).start()
                return c

            def tail(b, c, wait=wait):
                @pl.when(b >= nbu_ref[0])
                def _():
                    tail_fill(b).wait() if wait else tail_fill(b).start()
                return c

            lax.fori_loop(0, N_EXPERTS, experts, 0)
            lax.fori_loop(0, nb, tail, 0)

    def start_rows(r, c):
        for k in range(TOP_K):
            pltpu.make_async_copy(hp_ref.at[pl.ds(r, 1)], xs_hbm.at[pl.ds(dest_ref[0, TOP_K * r + k], 1)],
                                  sem).start(priority=k % 2)
        return c

    lax.fori_loop(0, tm, start_rows, 0, unroll=ROW_DMA_UNROLL)
    whole = xs_hbm.at[pl.ds(0, TOP_K * tm)]
    pltpu.make_async_copy(whole, whole, sem).wait()


def _dispatch(hp, dest, last_blk_row, nb_used, nb, tm=256):
    t, w = hp.shape
    tm = min(tm, t)
    rows = MOE_ROWS
    grid_spec = pltpu.PrefetchScalarGridSpec(
        num_scalar_prefetch=2,
        grid=(t // tm,),
        in_specs=[pl.BlockSpec((None, 1, TOP_K * tm), lambda i, lr, nu: (i, 0, 0), memory_space=pltpu.SMEM),
                  pl.BlockSpec((tm, w), lambda i, lr, nu: (i, 0))],
        out_specs=pl.BlockSpec(memory_space=pl.ANY),
        scratch_shapes=[pltpu.VMEM((rows, w), U32), pltpu.SemaphoreType.DMA(()), pltpu.SemaphoreType.DMA(())],
    )
    return pl.pallas_call(
        functools.partial(_dispatch_body, tm=tm, rows=rows, nb=nb),
        grid_spec=grid_spec,
        out_shape=jax.ShapeDtypeStruct((nb * rows, w), U32),
        compiler_params=_cparams(("arbitrary",), 16),
        name="moe_dispatch",
    )(last_blk_row, nb_used, dest.reshape(t // tm, 1, TOP_K * tm), hp)


def _expert_body(blk_exp_ref, nbu_ref, xs_ref, w1_ref, w3_ref, w2_ref, ys_ref):
    del blk_exp_ref
    b = pl.program_id(0)

    @pl.when(b < nbu_ref[0])
    def _():
        xn = _unpack_pairs(xs_ref[...]).astype(BF16)
        a1 = jnp.dot(xn, w1_ref[...], preferred_element_type=F32)
        a3 = jnp.dot(xn, w3_ref[...], preferred_element_type=F32)
        hmid = (jax.nn.silu(a1) * a3).astype(BF16)
        ys_ref[...] = _pack_pairs(jnp.dot(hmid, w2_ref[...], preferred_element_type=F32))

    @pl.when(b >= nbu_ref[0])
    def _():
        ys_ref[...] = jnp.zeros(ys_ref.shape, U32)


def _experts(xs, blk_exp, nb_used, w1, w3, w2):
    p, w = xs.shape
    _, d, de = w1.shape
    rows = MOE_ROWS
    grid_spec = pltpu.PrefetchScalarGridSpec(
        num_scalar_prefetch=2,
        grid=(p // rows,),
        in_specs=[
            pl.BlockSpec((rows, w), lambda b, be, nu: (jnp.minimum(b, nu[0] - 1), 0)),
            pl.BlockSpec((None, d, de), lambda b, be, nu: (be[b], 0, 0)),
            pl.BlockSpec((None, d, de), lambda b, be, nu: (be[b], 0, 0)),
            pl.BlockSpec((None, de, d), lambda b, be, nu: (be[b], 0, 0)),
        ],
        out_specs=pl.BlockSpec((rows, w), lambda b, be, nu: (b, 0)),
    )
    return pl.pallas_call(
        _expert_body,
        grid_spec=grid_spec,
        out_shape=jax.ShapeDtypeStruct((p, w), U32),
        compiler_params=_cparams(("arbitrary",), 40),
        name="moe_experts",
    )(blk_exp, nb_used, xs, w1, w3, w2)


def _combine_body(dest_ref, dest_next_ref, x_ref, gates_ref, g_ref, ys_hbm, *rest, tm, final):
    if final:
        o_ref, ybuf, sem = rest
    else:
        o_ref, h_ref, ybuf, sem = rest
    i = pl.program_id(0)
    slot = lax.rem(i, 2)

    def start_rows(d_ref, s):
        def rows_loop(r, c):
            for k in range(TOP_K):
                pltpu.make_async_copy(ys_hbm.at[pl.ds(d_ref[0, TOP_K * r + k], 1)], ybuf.at[s, k, pl.ds(r, 1)],
                                      sem.at[s]).start(priority=k % 2)
            return c

        lax.fori_loop(0, tm, rows_loop, 0, unroll=ROW_DMA_UNROLL)

    @pl.when(i == 0)
    def _():
        start_rows(dest_ref, 0)

    @pl.when(i + 1 < pl.num_programs(0))
    def _():
        start_rows(dest_next_ref, 1 - slot)

    pltpu.make_async_copy(ybuf.at[slot], ybuf.at[slot], sem.at[slot]).wait()
    gates = gates_ref[...]
    x = x_ref[...] + (gates[:, 0:1] * _unpack_pairs(ybuf[slot, 0]) + gates[:, 1:2] * _unpack_pairs(ybuf[slot, 1]))
    if final:
        o_ref[...] = _rms(x, g_ref[...])
    else:
        o_ref[...] = x
        h_ref[...] = _rms(x, g_ref[...]).astype(h_ref.dtype)


def _combine(x, ys, dest, gates, gain, final, tm=256):
    t, d = x.shape
    tm = min(tm, t)
    w = ys.shape[1]
    n = t // tm
    dest3 = dest.reshape(n, 1, TOP_K * tm)
    return pl.pallas_call(
        functools.partial(_combine_body, tm=tm, final=final),
        grid=(n,),
        in_specs=[pl.BlockSpec((None, 1, TOP_K * tm), lambda i: (i, 0, 0), memory_space=pltpu.SMEM),
                  pl.BlockSpec((None, 1, TOP_K * tm), lambda i: (jnp.minimum(i + 1, n - 1), 0, 0),
                               memory_space=pltpu.SMEM),
                  pl.BlockSpec((tm, d), lambda i: (i, 0)),
                  pl.BlockSpec((tm, LANES), lambda i: (i, 0)),
                  pl.BlockSpec((1, d), lambda i: (0, 0)),
                  pl.BlockSpec(memory_space=pl.ANY)],
        out_specs=[pl.BlockSpec((tm, d), lambda i: (i, 0))] * (1 if final else 2),
        out_shape=[jax.ShapeDtypeStruct((t, d), F32)] + ([] if final else [jax.ShapeDtypeStruct((t, d), BF16)]),
        scratch_shapes=[pltpu.VMEM((2, TOP_K, tm, w), U32), pltpu.SemaphoreType.DMA((2,))],
        compiler_params=_cparams(("arbitrary",), 48),
        name="moe_combine",
    )(dest3, dest3, x, gates, gain.reshape(1, d), ys)


def _split_w_in_body(wt_ref, lat_ref, qb_ref, kb_ref, vb_ref, gate_ref):
    wt = wt_ref[...]
    tr = wt.shape[1]
    d = gate_ref.shape[1] // 2
    o = np.cumsum([0, Q_LORA, KV_LORA, QK_ROPE, HB * 2 * DH_B, HB * 2 * DH_B, HB * 2 * DH_B, 2 * d])
    half = QK_ROPE // 2
    x1, x2 = wt[o[2]:o[2] + half], wt[o[2] + half:o[3]]
    pad = jnp.zeros((lat_ref.shape[1] - Q_LORA - KV_LORA - LANES, tr), wt.dtype)
    lat_ref[...] = jnp.concatenate([wt[:o[2]], x1, x2, x2, x1, pad], axis=0).T.astype(BF16)
    qb_ref[...] = wt[o[3]:o[4]].T.astype(BF16)
    kb_ref[...] = wt[o[4]:o[5]].T.astype(BF16)
    vb_ref[...] = wt[o[5]:o[6]].T.astype(BF16)
    gate_ref[...] = wt[o[6]:o[7]].T.astype(BF16)


def _split_w_in(w_in, tr=128):
    layers, d, d_in = w_in.shape
    wb = HB * 2 * DH_B
    widths = (2048, wb, wb, wb, 2 * d)
    return pl.pallas_call(
        _split_w_in_body,
        grid=(layers, d // tr),
        in_specs=[pl.BlockSpec((None, d_in, tr), lambda l, i: (l, 0, i))],
        out_specs=[pl.BlockSpec((None, tr, wd), lambda l, i: (l, i, 0)) for wd in widths],
        out_shape=[jax.ShapeDtypeStruct((layers, d, wd), BF16) for wd in widths],
        compiler_params=_cparams(("parallel", "parallel"), 48),
        name="split_w_in",
    )(jnp.swapaxes(w_in, 1, 2))


def _rope_cols(w_rope):
    half = QK_ROPE // 2
    x1, x2 = w_rope[..., :half], w_rope[..., half:]
    return jnp.concatenate([x1, x2, x2, x1], axis=-1)


def _layer_weights(w_uq, w_ukv, w_router_g, b_router_g, w_router_e, b_router_e):
    d = w_router_g.shape[0]
    wq = w_uq.reshape(Q_LORA, HA, QK_NOPE + QK_ROPE)
    wq = jnp.concatenate([wq[..., :QK_NOPE], _rope_cols(wq[..., QK_NOPE:])], axis=-1).reshape(Q_LORA, HA * 2 * LANES)
    w_r = jnp.concatenate([w_router_e, w_router_g, jnp.zeros((d, LANES - N_EXPERTS - N_GROUPS), F32)], axis=1)
    w_r_hi = w_r.astype(BF16)
    w_r_lo = (w_r - w_r_hi.astype(F32)).astype(BF16)
    b_r = jnp.concatenate([b_router_e, b_router_g, jnp.zeros((LANES - N_EXPERTS - N_GROUPS,), F32)]).reshape(1, LANES)
    return wq.astype(BF16), w_ukv.astype(BF16), w_r_hi, w_r_lo, b_r


def kernel(x, positions, norm_attn, w_in, q_norm, w_uq, kv_norm, w_ukv, lam_q1, lam_k1, lam_q2, lam_k2, subln, w_oa,
           w_ob, w_out, norm_ffn, w_router_g, b_router_g, w_router_e, b_router_e, w1, w3, w2, norm_final):
    bsz, s, d = x.shape
    assert bsz == 1
    depth = w_in.shape[0]
    x = x.reshape(s, d)
    pos_col = positions.reshape(s, 1)
    table = _rope_table(pos_col)
    pos_col_f = pos_col.astype(F32)
    pos_row_f = pos_col_f.reshape(1, s)
    slopes_l2 = jnp.asarray(2.0 ** (-8.0 * (np.arange(HB) + 1) / HB) * LOG2E, dtype=F32)
    c_a = (QK_NOPE + QK_ROPE) ** -0.5 * LOG2E
    c_b = DH_B ** -0.5 * LOG2E
    tm = min(1024, s)
    tk_a, tk_b = min(ATT_TK_A, s), min(ATT_TK_B, s)
    hd = 2 * LANES

    def chunked_transpose(acc, tk):
        t = jnp.concatenate([acc.T, jnp.ones((ONES_ROWS, tm), F32)], axis=0)
        return jnp.stack([t[:, c * tk:(c + 1) * tk] for c in range(tm // tk)])

    w_lat, w_qb, w_kb, w_vb, w_gate = _split_w_in(w_in)
    n_exp = w1.shape[1]
    w1b, w3b, w2b = (w.reshape((depth * n_exp,) + w.shape[2:]).astype(BF16) for w in (w1, w3, w2))

    for l in range(depth):
        wq, wkv, w_r_hi, w_r_lo, b_r = _layer_weights(
            w_uq[l], w_ukv[l], w_router_g[l], b_router_g[l], w_router_e[l], b_router_e[l])
        if l == 0:
            h = _rmsnorm(x, norm_attn[l], BF16)
        full = ((tm, d), lambda i, j: (i, 0))

        def ident(acc):
            return (acc,)

        (lat,) = _matmul("in_proj_latent", h, *full, w_lat, 512,
                         [(jax.ShapeDtypeStruct((s, 2048), BF16), (tm, 512), lambda i, j: (i, j))], ident, b_layer=l)

        (qt_b,) = _matmul("in_proj_q_b", h, *full, w_qb, hd,
                          [(jax.ShapeDtypeStruct((HB, hd, s), BF16), (None, hd, tm), lambda i, j: (j, 0, i))],
                          lambda acc: ((acc * c_b).T,), b_layer=l)
        (k_b,) = _matmul("in_proj_k_b", h, *full, w_kb, hd,
                         [(jax.ShapeDtypeStruct((HB, s, hd), BF16), (None, tm, hd), lambda i, j: (j, i, 0))], ident,
                         b_layer=l)
        (vt_b,) = _matmul("in_proj_v_b", h, *full, w_vb, hd,
                          [(jax.ShapeDtypeStruct((HB, s // tk_b, hd + ONES_ROWS, tk_b), BF16),
                            (None, tm // tk_b, hd + ONES_ROWS, tk_b),
                            lambda i, j: (j, i, 0, 0))], lambda acc: (chunked_transpose(acc, tk_b),), b_layer=l)
        (gates,) = _matmul("in_proj_gates", h, *full, w_gate, 512,
                           [(jax.ShapeDtypeStruct((s, 2 * d), BF16), (tm, 512), lambda i, j: (i, j))], ident,
                           b_layer=l)

        def q_epilogue(acc, t_ref):
            heads = []
            for hh in range(2):
                a = acc[:, hh * hd:(hh + 1) * hd]
                rot = _rope_128(a[:, LANES:], t_ref[...])
                heads.append((jnp.concatenate([a[:, :LANES], rot], axis=1) * c_a).T)
            return (jnp.stack(heads),)

        (qt_a,) = _matmul("q_up_proj", lat, (tm, Q_LORA), lambda i, j: (i, 0), wq, 2 * hd,
                          [(jax.ShapeDtypeStruct((HA, hd, s), BF16), (2, hd, tm), lambda i, j: (j, 0, i))],
                          q_epilogue, gain=q_norm[l], extras=[(table, (tm, LANES), lambda i, j: (i, 0))])

        def kv_epilogue(acc, kr_ref, t_ref):
            rot = _rope_128(kr_ref[...].astype(F32), t_ref[...])
            lane = lax.broadcasted_iota(I32, rot.shape, 1)
            rot = jnp.where(lane < QK_ROPE, rot, 0.0)
            return jnp.concatenate([acc[:, :LANES], rot], axis=1), chunked_transpose(acc[:, LANES:], tk_a)

        k_a, vt_a = _matmul("kv_up_proj", lat, (tm, KV_LORA), lambda i, j: (i, Q_LORA // KV_LORA), wkv, hd,
                            [(jax.ShapeDtypeStruct((HA, s, hd), BF16), (None, tm, hd), lambda i, j: (j, i, 0)),
                             (jax.ShapeDtypeStruct((HA, s // tk_a, V_DIM_A + ONES_ROWS, tk_a), BF16),
                              (None, tm // tk_a, V_DIM_A + ONES_ROWS, tk_a), lambda i, j: (j, i, 0, 0))],
                            kv_epilogue, gain=kv_norm[l],
                            extras=[(lat, (tm, LANES), lambda i, j: (i, (Q_LORA + KV_LORA) // LANES)),
                                    (table, (tm, LANES), lambda i, j: (i, 0))])
        o_a = _mla_attention(qt_a, k_a, vt_a)

        lam_init = 0.8 - 0.6 * math.exp(-0.3 * l)
        lam_vecs = [v[l].reshape(1, DH_B).astype(F32) for v in (lam_q1, lam_k1, lam_q2, lam_k2)]
        o_b = _diff_attention(qt_b, k_b, vt_b, pos_col_f, pos_row_f, slopes_l2, lam_vecs, subln[l], lam_init)

        merged = _merge(o_a, o_b, w_oa[l].astype(BF16), w_ob[l].astype(BF16), gates)

        def residual(acc, x_ref):
            return (x_ref[...] + acc,)

        (x,) = _matmul("out_proj", merged, (tm, d), lambda i, j: (i, 0), w_out[l].astype(BF16), 512,
                       [(jax.ShapeDtypeStruct((s, d), F32), (tm, 512), lambda i, j: (i, j))], residual,
                       extras=[(x, (tm, 512), lambda i, j: (i, j))])

        ids, gate_vals, hp, counts = _router(x, norm_ffn[l], w_r_hi, w_r_lo, b_r)
        dest, blk_exp, nb_used, last_blk_row = _dispatch_plan(ids, counts, s)
        xs = _dispatch(hp, dest, last_blk_row, nb_used, blk_exp.shape[0])
        ys = _experts(xs, blk_exp + l * n_exp, nb_used, w1b, w3b, w2b)
        final = l == depth - 1
        x, *h_next = _combine(x, ys, dest, gate_vals, norm_final if final else norm_attn[l + 1], final)
        h = h_next[0] if h_next else None
    return x.reshape(bsz, s, d)
```
